```python
import math
import jax
import jax.numpy as jnp
from jax import lax
import numpy as np

D_MODEL = 1024
BATCH = 32
SEQ = 256
DEPTH = 4
DEC_BATCH = 2
DEC_SEQ = 4096
PAST_LEN = 512

GRID_W = 64
N_MIXERS = 3
HEAD_DIM = 64
QB = 128
ROPE_THETA = 10000.0
EPS = 1e-6
NEG_INF = -1e30
A_Q_HEADS = D_MODEL // HEAD_DIM
A_KV_HEADS = A_Q_HEADS // 4
A_GROUP = A_Q_HEADS // A_KV_HEADS
A_WINDOW = 128
B_HEADS = D_MODEL // (2 * HEAD_DIM)
C_HEADS = D_MODEL // HEAD_DIM
NA_ROWS = 8
NA_COLS = 16
PEER_HEADS = 8
PEER_NKEYS = 128
PEER_EXPERTS = PEER_NKEYS * PEER_NKEYS
PEER_DK = 256
PEER_DHALF = PEER_DK // 2
PEER_TOPK = 16
PEER_BLOCK = 128
N_A = len(range(0, DEPTH, N_MIXERS))
N_B = len(range(1, DEPTH, N_MIXERS))
N_C = len(range(2, DEPTH, N_MIXERS))

kernel_name = 'hybrid_flow_prefix_trunk'


def rms_norm(x, gain):
    xf = x.astype(jnp.float32)
    y = xf * lax.rsqrt(jnp.mean(xf * xf, axis=-1, keepdims=True) + EPS)
    return (y * gain.astype(jnp.float32)).astype(x.dtype)


def modulation(cond, w_ada, b_ada):
    m = jax.nn.silu(cond) @ w_ada + b_ada
    return [t[:, None, :] for t in jnp.split(m, 6, axis=-1)]


def rotate_half(x):
    x1, x2 = jnp.split(x, 2, axis=-1)
    return jnp.concatenate([-x2, x1], axis=-1)


def axial_rope(n_tokens, dim):
    pos = jnp.arange(n_tokens)
    row = (pos // GRID_W).astype(jnp.float32)
    col = (pos % GRID_W).astype(jnp.float32)
    half = dim // 2
    inv_freq = 1.0 / (ROPE_THETA ** (jnp.arange(0, half, 2, dtype=jnp.float32) / half))
    ang_r = row[:, None] * inv_freq[None, :]
    ang_c = col[:, None] * inv_freq[None, :]
    ang = jnp.concatenate([ang_r, ang_r, ang_c, ang_c], axis=-1)
    return jnp.cos(ang), jnp.sin(ang)


def apply_rope(x, cos, sin):
    shape = (1, x.shape[1]) + (1,) * (x.ndim - 3) + (x.shape[-1],)
    xf = x.astype(jnp.float32)
    x_row, x_col = jnp.split(xf, 2, axis=-1)
    rot = jnp.concatenate([rotate_half(x_row), rotate_half(x_col)], axis=-1)
    return (xf * cos.reshape(shape) + rot * sin.reshape(shape)).astype(x.dtype)


def to_blocks(x, n):
    b, l = x.shape[:2]
    return jnp.moveaxis(x.reshape((b, l // n, n) + x.shape[2:]), 1, 0)


def from_blocks(x):
    nb, b, n = x.shape[:3]
    return jnp.moveaxis(x, 0, 1).reshape((b, nb * n) + x.shape[3:])


def joint_attend(q, keys, values, biases, sink=None):
    scale = q.shape[-1] ** -0.5
    logits = []
    for k, b in zip(keys, biases):
        s = jnp.einsum('bqhgd,bkhd->bhgqk', q, k).astype(jnp.float32) * scale
        logits.append(s if b is None else s + b)
    sizes = [s.shape[-1] for s in logits]
    if sink is not None:
        logits.append(jnp.broadcast_to(sink.astype(jnp.float32)[None, :, :, None, None],
                                       logits[0].shape[:-1] + (1,)))
    p = jax.nn.softmax(jnp.concatenate(logits, axis=-1), axis=-1)
    out = None
    start = 0
    for v, n in zip(values, sizes):
        o = jnp.einsum('bhgqk,bkhd->bqhgd', p[..., start:start + n].astype(v.dtype), v)
        out = o if out is None else out + o
        start += n
    return out


def context_attend(q, k, v, sink=None):
    out = lax.map(lambda qb: joint_attend(qb, [k], [v], [None], sink), to_blocks(q, QB))
    return from_blocks(out)


def a_project(h, w_qkv, q_gain, k_gain):
    b, l, _ = h.shape
    q, k, v = jnp.split(h @ w_qkv, [A_Q_HEADS * HEAD_DIM, (A_Q_HEADS + A_KV_HEADS) * HEAD_DIM], axis=-1)
    q = rms_norm(q.reshape(b, l, A_KV_HEADS, A_GROUP, HEAD_DIM), q_gain)
    k = rms_norm(k.reshape(b, l, A_KV_HEADS, HEAD_DIM), k_gain)
    v = v.reshape(b, l, A_KV_HEADS, HEAD_DIM)
    return q, k, v


def a_context(h, w_qkv, w_o, q_gain, k_gain, sink):
    b, l, _ = h.shape
    q, k, v = a_project(h, w_qkv, q_gain, k_gain)
    o = context_attend(q, k, v, sink.reshape(A_KV_HEADS, A_GROUP))
    return o.reshape(b, l, -1) @ w_o, (k, v)


def a_latent(h, k_ctx, v_ctx, w_qkv, w_o, q_gain, k_gain, sink):
    b, l, _ = h.shape
    q, k, v = a_project(h, w_qkv, q_gain, k_gain)
    cos, sin = axial_rope(l, HEAD_DIM)
    q = apply_rope(q, cos, sin)
    k = apply_rope(k, cos, sin)
    pad = ((0, 0), (QB, QB), (0, 0), (0, 0))
    k_pad, v_pad = jnp.pad(k, pad), jnp.pad(v, pad)
    sink_g = sink.reshape(A_KV_HEADS, A_GROUP)
    q_off = jnp.arange(QB)[:, None]
    k_off = jnp.arange(3 * QB)[None, :] - QB
    band = jnp.abs(q_off - k_off) <= A_WINDOW

    def block(args):
        qb, n = args
        k_loc = lax.dynamic_slice_in_dim(k_pad, n * QB, 3 * QB, axis=1)
        v_loc = lax.dynamic_slice_in_dim(v_pad, n * QB, 3 * QB, axis=1)
        k_pos = n * QB + k_off
        bias = jnp.where(band & (k_pos >= 0) & (k_pos < l), 0.0, NEG_INF)
        return joint_attend(qb, [k_loc, k_ctx], [v_loc, v_ctx], [bias, None], sink_g)

    o = from_blocks(lax.map(block, (to_blocks(q, QB), jnp.arange(l // QB))))
    return o.reshape(b, l, -1) @ w_o


def b_project(h, w_qkv, q_gain, k_gain):
    b, l, _ = h.shape
    q, k, v = jnp.split(h @ w_qkv, 3, axis=-1)
    q = rms_norm(q.reshape(b, l, B_HEADS, 2, HEAD_DIM), q_gain)
    k = rms_norm(k.reshape(b, l, B_HEADS, 2, HEAD_DIM), k_gain)
    v = v.reshape(b, l, B_HEADS, 2 * HEAD_DIM)
    return q, k, v


def diff_lambda(lam, layer):
    lam = lam.astype(jnp.float32)
    init = 0.8 - 0.6 * math.exp(-0.3 * layer)
    return jnp.exp(jnp.sum(lam[0] * lam[1])) - jnp.exp(jnp.sum(lam[2] * lam[3])) + init, init


def diff_attend(qb, keys1, keys2, values, biases, lam):
    o1 = joint_attend(qb[:, :, :, 0:1], keys1, values, biases)
    o2 = joint_attend(qb[:, :, :, 1:2], keys2, values, biases)
    return o1 - lam.astype(o1.dtype) * o2


def b_output(o, sub_gain, init, w_o):
    b, l = o.shape[:2]
    o = rms_norm(o.reshape(b, l, B_HEADS, 2 * HEAD_DIM), sub_gain) * (1.0 - init)
    return o.reshape(b, l, -1) @ w_o


def b_context(h, w_qkv, w_o, q_gain, k_gain, lam_p, sub_gain, layer):
    b, l, _ = h.shape
    q, k, v = b_project(h, w_qkv, q_gain, k_gain)
    lam, init = diff_lambda(lam_p, layer)
    k1, k2 = k[:, :, :, 0], k[:, :, :, 1]
    o = from_blocks(lax.map(lambda qb: diff_attend(qb, [k1], [k2], [v], [None], lam), to_blocks(q, QB)))
    return b_output(o, sub_gain, init, w_o), (k.reshape(b, l, B_HEADS, 2 * HEAD_DIM), v)


def b_latent(h, k_ctx, v_ctx, w_qkv, w_o, q_gain, k_gain, lam_p, sub_gain, layer):
    b, l, _ = h.shape
    q, k, v = b_project(h, w_qkv, q_gain, k_gain)
    cos, sin = axial_rope(l, HEAD_DIM)
    q = apply_rope(q, cos, sin)
    k = apply_rope(k, cos, sin)
    lam, init = diff_lambda(lam_p, layer)
    k1, k2 = k[:, :, :, 0], k[:, :, :, 1]
    k1c, k2c = k_ctx[..., :HEAD_DIM], k_ctx[..., HEAD_DIM:]
    o = from_blocks(lax.map(
        lambda qb: diff_attend(qb, [k1, k1c], [k2, k2c], [v, v_ctx], [None, None], lam),
        to_blocks(q, QB)))
    return b_output(o, sub_gain, init, w_o)


def c_project(h, w_qkv, q_gain, k_gain):
    b, l, _ = h.shape
    q, k, v = jnp.split(h @ w_qkv, 3, axis=-1)
    q = rms_norm(q.reshape(b, l, C_HEADS, 1, HEAD_DIM), q_gain)
    k = rms_norm(k.reshape(b, l, C_HEADS, HEAD_DIM), k_gain)
    v = v.reshape(b, l, C_HEADS, HEAD_DIM)
    return q, k, v


def c_context(h, w_qkv, w_o, q_gain, k_gain):
    b, l, _ = h.shape
    q, k, v = c_project(h, w_qkv, q_gain, k_gain)
    o = context_attend(q, k, v)
    return o.reshape(b, l, -1) @ w_o, (k, v)


def c_latent(h, k_ctx, v_ctx, w_qkv, w_o, q_gain, k_gain, rpb):
    b, l, _ = h.shape
    rows = l // GRID_W
    wr = min(NA_ROWS, rows)
    q, k, v = c_project(h, w_qkv, q_gain, k_gain)
    k_grid = k.reshape(b, rows, GRID_W, C_HEADS, HEAD_DIM)
    v_grid = v.reshape(b, rows, GRID_W, C_HEADS, HEAD_DIM)
    cols = jnp.arange(GRID_W)
    c_start = jnp.clip(cols - NA_COLS // 2, 0, GRID_W - NA_COLS)
    col_mask = (cols[None, :] >= c_start[:, None]) & (cols[None, :] < c_start[:, None] + NA_COLS)
    dc = jnp.clip(cols[None, :] - cols[:, None] + NA_COLS - 1, 0, 2 * NA_COLS - 2)
    rpb_f = rpb.astype(jnp.float32)

    def row_block(args):
        qr, r = args
        r_start = jnp.clip(r - wr // 2, 0, rows - wr)
        k_rows = lax.dynamic_slice_in_dim(k_grid, r_start, wr, axis=1).reshape(b, wr * GRID_W, C_HEADS, HEAD_DIM)
        v_rows = lax.dynamic_slice_in_dim(v_grid, r_start, wr, axis=1).reshape(b, wr * GRID_W, C_HEADS, HEAD_DIM)
        dr = r_start + jnp.arange(wr) - r + NA_ROWS - 1
        bias = rpb_f[:, dr][:, :, dc]
        bias = jnp.where(col_mask[None, None], bias, NEG_INF)
        bias = jnp.transpose(bias, (0, 2, 1, 3)).reshape(C_HEADS, GRID_W, wr * GRID_W)[None, :, None]
        return joint_attend(qr, [k_rows, k_ctx], [v_rows, v_ctx], [bias, None])

    q_rows = jnp.moveaxis(q.reshape(b, rows, GRID_W, C_HEADS, 1, HEAD_DIM), 1, 0)
    o = from_blocks(lax.map(row_block, (q_rows, jnp.arange(rows))))
    return o.reshape(b, l, -1) @ w_o


def peer_ffn(h, w_query, sub_keys, u, v):
    b, l, d = h.shape
    t = b * l
    x = h.reshape(t, d)
    q = (x @ w_query).reshape(t, PEER_HEADS, 2, PEER_DHALF)
    s = jnp.einsum('thpd,hpkd->thpk', q, sub_keys).astype(jnp.float32)
    top_s, top_i = lax.top_k(s, PEER_TOPK)
    cand_s = (top_s[:, :, 0, :, None] + top_s[:, :, 1, None, :]).reshape(t, PEER_HEADS, PEER_TOPK * PEER_TOPK)
    cand_i = (top_i[:, :, 0, :, None] * PEER_NKEYS + top_i[:, :, 1, None, :]).reshape(t, PEER_HEADS, PEER_TOPK * PEER_TOPK)
    best_s, best_pos = lax.top_k(cand_s, PEER_TOPK)
    experts = jnp.take_along_axis(cand_i, best_pos, axis=-1)
    gates = jax.nn.softmax(best_s, axis=-1).astype(h.dtype)

    def expert_block(args):
        xb, eb, gb = args
        act = jax.nn.gelu(jnp.einsum('td,thkd->thk', xb, u[eb]), approximate=False)
        return jnp.einsum('thk,thkd->td', gb * act, v[eb])

    nb = t // PEER_BLOCK
    out = lax.map(expert_block, (x.reshape(nb, PEER_BLOCK, d),
                                 experts.reshape(nb, PEER_BLOCK, PEER_HEADS, PEER_TOPK),
                                 gates.reshape(nb, PEER_BLOCK, PEER_HEADS, PEER_TOPK)))
    return out.reshape(b, l, d)


def residual_layer(x, cond, mixer, channel_mixer, g_mix, g_ffn, w_ada, b_ada):
    sh1, sc1, gt1, sh2, sc2, gt2 = modulation(cond, w_ada, b_ada)
    mix_out, aux = mixer(rms_norm(x, g_mix) * (1.0 + sc1) + sh1)
    x = x + gt1 * mix_out
    x = x + gt2 * channel_mixer(rms_norm(x, g_ffn) * (1.0 + sc2) + sh2)
    return x, aux


def setup_inputs(seed: int = 0) -> dict:
    key = jax.random.key(seed)
    keys = iter(jax.random.split(key, 40))
    D = D_MODEL
    HD = HEAD_DIM

    def nrm(shape, scale):
        return jax.random.normal(next(keys), shape, jnp.float32) * scale

    def gain(shape):
        return 1.0 + nrm(shape, 0.02)

    return {
        'x_prompt': nrm((BATCH, SEQ, D), 1.0),
        'x_sample': nrm((DEC_BATCH, DEC_SEQ, D), 1.0),
        'cache_a_k': nrm((DEC_BATCH, N_A, PAST_LEN, A_KV_HEADS, HD), 1.0),
        'cache_a_v': nrm((DEC_BATCH, N_A, PAST_LEN, A_KV_HEADS, HD), 1.0),
        'cache_b_k': nrm((DEC_BATCH, N_B, PAST_LEN, B_HEADS, 2 * HD), 1.0),
        'cache_b_v': nrm((DEC_BATCH, N_B, PAST_LEN, B_HEADS, 2 * HD), 1.0),
        'cache_c_k': nrm((DEC_BATCH, N_C, PAST_LEN, C_HEADS, HD), 1.0),
        'cache_c_v': nrm((DEC_BATCH, N_C, PAST_LEN, C_HEADS, HD), 1.0),
        'c': nrm((DEC_BATCH, D), 1.0),
        'c_ctx': nrm((D,), 1.0),
        'norm_mix': gain((DEPTH, D)),
        'norm_ffn': gain((DEPTH, D)),
        'ada_w': nrm((DEPTH, D, 6 * D), 0.5 * D ** -0.5),
        'ada_b': nrm((DEPTH, 6 * D), 0.01),
        'a_w_qkv': nrm((N_A, D, (A_Q_HEADS + 2 * A_KV_HEADS) * HD), D ** -0.5),
        'a_w_o': nrm((N_A, A_Q_HEADS * HD, D), (A_Q_HEADS * HD) ** -0.5),
        'a_q_norm': gain((N_A, HD)),
        'a_k_norm': gain((N_A, HD)),
        'a_sink': nrm((N_A, A_Q_HEADS), 0.5),
        'b_w_qkv': nrm((N_B, D, 3 * B_HEADS * 2 * HD), D ** -0.5),
        'b_w_o': nrm((N_B, B_HEADS * 2 * HD, D), (B_HEADS * 2 * HD) ** -0.5),
        'b_q_norm': gain((N_B, HD)),
        'b_k_norm': gain((N_B, HD)),
        'b_lambda': nrm((N_B, 4, HD), 0.1),
        'b_sub_norm': gain((N_B, 2 * HD)),
        'c_w_qkv': nrm((N_C, D, 3 * C_HEADS * HD), D ** -0.5),
        'c_w_o': nrm((N_C, C_HEADS * HD, D), (C_HEADS * HD) ** -0.5),
        'c_q_norm': gain((N_C, HD)),
        'c_k_norm': gain((N_C, HD)),
        'c_rpb': nrm((N_C, C_HEADS, 2 * NA_ROWS - 1, 2 * NA_COLS - 1), 0.5),
        'peer_w_query': nrm((DEPTH, D, PEER_HEADS * PEER_DK), D ** -0.5),
        'peer_sub_keys': nrm((DEPTH, PEER_HEADS, 2, PEER_NKEYS, PEER_DHALF), PEER_DHALF ** -0.5),
        'peer_u': nrm((DEPTH, PEER_EXPERTS, D), D ** -0.5),
        'peer_v': nrm((DEPTH, PEER_EXPERTS, D), 0.5),
    }


def reference(x_prompt, x_sample, cache_a_k, cache_a_v, cache_b_k, cache_b_v, cache_c_k, cache_c_v,
              c, c_ctx, norm_mix, norm_ffn, ada_w, ada_b,
              a_w_qkv, a_w_o, a_q_norm, a_k_norm, a_sink,
              b_w_qkv, b_w_o, b_q_norm, b_k_norm, b_lambda, b_sub_norm,
              c_w_qkv, c_w_o, c_q_norm, c_k_norm, c_rpb,
              peer_w_query, peer_sub_keys, peer_u, peer_v):
    def ffn(l):
        return lambda h: peer_ffn(h, peer_w_query[l], peer_sub_keys[l], peer_u[l], peer_v[l])

    y_prompt = x_prompt
    ctx_cond = c_ctx[None, :]
    new_k = ([], [], [])
    new_v = ([], [], [])
    for l in range(DEPTH):
        kind, j = l % N_MIXERS, l // N_MIXERS
        if kind == 0:
            mix = lambda h: a_context(h, a_w_qkv[j], a_w_o[j], a_q_norm[j], a_k_norm[j], a_sink[j])
        elif kind == 1:
            mix = lambda h: b_context(h, b_w_qkv[j], b_w_o[j], b_q_norm[j], b_k_norm[j], b_lambda[j], b_sub_norm[j], l)
        else:
            mix = lambda h: c_context(h, c_w_qkv[j], c_w_o[j], c_q_norm[j], c_k_norm[j])
        y_prompt, (k, v) = residual_layer(y_prompt, ctx_cond, mix, ffn(l), norm_mix[l], norm_ffn[l], ada_w[l], ada_b[l])
        new_k[kind].append(k)
        new_v[kind].append(v)

    y_sample = x_sample
    for l in range(DEPTH):
        kind, j = l % N_MIXERS, l // N_MIXERS
        if kind == 0:
            mix = lambda h: (a_latent(h, cache_a_k[:, j], cache_a_v[:, j], a_w_qkv[j], a_w_o[j],
                                      a_q_norm[j], a_k_norm[j], a_sink[j]), None)
        elif kind == 1:
            mix = lambda h: (b_latent(h, cache_b_k[:, j], cache_b_v[:, j], b_w_qkv[j], b_w_o[j],
                                      b_q_norm[j], b_k_norm[j], b_lambda[j], b_sub_norm[j], l), None)
        else:
            mix = lambda h: (c_latent(h, cache_c_k[:, j], cache_c_v[:, j], c_w_qkv[j], c_w_o[j],
                                      c_q_norm[j], c_k_norm[j], c_rpb[j]), None)
        y_sample, _ = residual_layer(y_sample, c, mix, ffn(l), norm_mix[l], norm_ffn[l], ada_w[l], ada_b[l])

    new_a_k = jnp.stack(new_k[0], axis=1)
    new_a_v = jnp.stack(new_v[0], axis=1)
    new_b_k = jnp.stack(new_k[1], axis=1)
    new_b_v = jnp.stack(new_v[1], axis=1)
    new_c_k = jnp.stack(new_k[2], axis=1)
    new_c_v = jnp.stack(new_v[2], axis=1)
    return (y_prompt, y_sample, new_a_k, new_a_v, new_b_k, new_b_v, new_c_k, new_c_v)
```

```python
import functools
import math

import numpy as np
import jax
import jax.numpy as jnp
from jax import lax
from jax.experimental import pallas as pl
from jax.experimental.pallas import tpu as pltpu

F32 = jnp.float32
BF16 = jnp.bfloat16
HIGHEST = lax.Precision.HIGHEST

D_MODEL = 1024
BATCH = 32
SEQ = 256
DEPTH = 4
DEC_BATCH = 2
DEC_SEQ = 4096
PAST_LEN = 512
GRID_W = 64
N_MIXERS = 3
HEAD_DIM = 64
ROPE_THETA = 10000.0
EPS = 1e-6
NEG_INF = -1e30
A_WINDOW = 128
QB = 128
NA_ROWS = 8
NA_COLS = 16
PEER_HEADS = 8
PEER_NKEYS = 128
PEER_EXPERTS = PEER_NKEYS * PEER_NKEYS
PEER_TOPK = 16

N_CTX = BATCH * SEQ
N_LAT = DEC_BATCH * DEC_SEQ
N_TOK = N_CTX + N_LAT

LANES = 128
SUBLANES = 8
VMEM_LIMIT = 56 * 1024 * 1024

TM = 512
CTX_TILES = N_CTX // TM
LAT_TILES = N_LAT // TM
LAT_TILES_PER_BATCH = DEC_SEQ // TM

PM_TM = 256
PM_EC = 2048
PM_ROWS = PM_EC // PEER_NKEYS
SLAB_PITCH = 136

SQRT_HALF = float(np.sqrt(0.5))


def _cparams(sem):
    return pltpu.CompilerParams(dimension_semantics=sem, vmem_limit_bytes=VMEM_LIMIT)


MOD_TN = 1536


def _mod_kernel(c_ref, w_ref, b_ref, o_ref):
    c = c_ref[...]
    s = c * (1.0 / (1.0 + jnp.exp(-c)))
    o_ref[0] = jnp.dot(s, w_ref[0], precision=HIGHEST, preferred_element_type=F32) + b_ref[0]


def _modulation(cond8, ada_w, ada_b):
    nt = (6 * D_MODEL) // MOD_TN
    return pl.pallas_call(
        _mod_kernel,
        grid=(DEPTH, nt),
        in_specs=[
            pl.BlockSpec((SUBLANES, D_MODEL), lambda l, n: (0, 0)),
            pl.BlockSpec((1, D_MODEL, MOD_TN), lambda l, n: (l, 0, n)),
            pl.BlockSpec((1, 1, MOD_TN), lambda l, n: (l, 0, n)),
        ],
        out_specs=pl.BlockSpec((1, SUBLANES, MOD_TN), lambda l, n: (l, 0, n)),
        out_shape=jax.ShapeDtypeStruct((DEPTH, SUBLANES, 6 * D_MODEL), F32),
        compiler_params=_cparams(("arbitrary", "arbitrary")),
        name="modulation",
    )(cond8, ada_w, ada_b.reshape(DEPTH, 1, 6 * D_MODEL))


def _mod_row_of_tile(i, latent):
    return 1 + i // LAT_TILES_PER_BATCH if latent else 0


def _modnorm(x, gain, scale, shift):
    y = x * lax.rsqrt(jnp.mean(x * x, axis=-1, keepdims=True) + EPS)
    return (y * gain) * (1.0 + scale) + shift


def _proj_kernel(*refs, nq, nk, nv, swap, latent, use_rope):
    if use_rope:
        (x_ref, mod_ref, g_ref, w_ref, gq_ref, gk_ref, bd_ref, cos_ref, sa_ref, sb_ref, out_ref) = refs
    elif latent:
        (x_ref, mod_ref, g_ref, w_ref, gq_ref, gk_ref, bd_ref, out_ref) = refs
    else:
        (x_ref, mod_ref, g_ref, w_ref, gq_ref, gk_ref, bd_ref, out_ref, kf_ref, vf_ref) = refs
    h = _modnorm(x_ref[...], g_ref[...], mod_ref[0, 1:2, :], mod_ref[0, 0:1, :])
    y = jnp.dot(h.astype(BF16), w_ref[...], preferred_element_type=F32)
    bd = bd_ref[...]
    scale = HEAD_DIM ** -0.5

    def qk_norm(c, gain_row):
        yc = y[:, c * LANES:(c + 1) * LANES]
        ss = jnp.dot(yc * yc, bd, precision=HIGHEST, preferred_element_type=F32)
        yn = yc * lax.rsqrt(ss * (1.0 / HEAD_DIM) + EPS) * gain_row
        if use_rope:
            yn = (yn * cos_ref[...] + pltpu.roll(yn, LANES - 16, 1) * sa_ref[...]
                  + pltpu.roll(yn, 16, 1) * sb_ref[...])
        return yn

    col = 0
    for c in range(nq):
        out_ref[:, col * LANES:(col + 1) * LANES] = (qk_norm(c, gq_ref[...]) * scale).astype(BF16)
        col += 1
    ks = [qk_norm(nq + c, gk_ref[...]) for c in range(nk)]
    vs = [y[:, (nq + nk + c) * LANES:(nq + nk + c + 1) * LANES] for c in range(nv)]
    groups = [ks] + ([[pltpu.roll(k, HEAD_DIM, 1) for k in ks]] if swap else [])
    groups += [vs] + ([[pltpu.roll(v, HEAD_DIM, 1) for v in vs]] if swap else [])
    for grp in groups:
        for a in grp:
            out_ref[:, col * LANES:(col + 1) * LANES] = a.astype(BF16)
            col += 1
    if not latent:
        for c in range(nk):
            kf_ref[:, c * LANES:(c + 1) * LANES] = ks[c]
        for c in range(nv):
            vf_ref[:, c * LANES:(c + 1) * LANES] = vs[c]


def _project(x_all, mod_l, gain, w_bf, gq2, gk2, bd, rope, *, nq, nk, nv, swap, latent):
    n_in = (nq + nk + nv) * LANES
    n_out = (nq + (2 if swap else 1) * (nk + nv)) * LANES
    tile0 = CTX_TILES if latent else 0
    const = lambda i: (0, 0)
    in_specs = [
        pl.BlockSpec((TM, D_MODEL), lambda i: (i + tile0, 0)),
        pl.BlockSpec((1, 6, D_MODEL), lambda i: (_mod_row_of_tile(i, latent), 0, 0)),
        pl.BlockSpec((1, D_MODEL), const),
        pl.BlockSpec((D_MODEL, n_in), const),
        pl.BlockSpec((1, LANES), const),
        pl.BlockSpec((1, LANES), const),
        pl.BlockSpec((LANES, LANES), const),
    ]
    args = [x_all, mod_l, gain, w_bf, gq2, gk2, bd]
    use_rope = rope is not None
    if use_rope:
        rope_spec = pl.BlockSpec((TM, LANES), lambda i: (i % LAT_TILES_PER_BATCH, 0))
        in_specs += [rope_spec] * 3
        args += list(rope)
    if latent:
        out_specs = pl.BlockSpec((TM, n_out), lambda i: (i, 0))
        out_shape = jax.ShapeDtypeStruct((N_LAT, n_out), BF16)
    else:
        out_specs = [pl.BlockSpec((TM, n_out), lambda i: (i, 0)),
                     pl.BlockSpec((TM, nk * LANES), lambda i: (i, 0)),
                     pl.BlockSpec((TM, nv * LANES), lambda i: (i, 0))]
        out_shape = [jax.ShapeDtypeStruct((N_CTX, n_out), BF16),
                     jax.ShapeDtypeStruct((N_CTX, nk * LANES), F32),
                     jax.ShapeDtypeStruct((N_CTX, nv * LANES), F32)]
    return pl.pallas_call(
        functools.partial(_proj_kernel, nq=nq, nk=nk, nv=nv, swap=swap, latent=latent, use_rope=use_rope),
        grid=(LAT_TILES if latent else CTX_TILES,),
        in_specs=in_specs, out_specs=out_specs, out_shape=out_shape,
        compiler_params=_cparams(("arbitrary",)),
        name="qkv_proj_lat" if latent else "qkv_proj_ctx",
    )(*args)


def _qk(q, k):
    return lax.dot_general(q, k, (((1,), (1,)), ((), ())), preferred_element_type=F32)


def _softmax_pv(logit_sets, value_sets, sink=None):
    m = None
    for s in logit_sets:
        r = jnp.max(s, axis=-1, keepdims=True)
        m = r if m is None else jnp.maximum(m, r)
    if sink is not None:
        m = jnp.maximum(m, sink)
    l = None
    o = None
    for s, v in zip(logit_sets, value_sets):
        p = jnp.exp(s - m)
        ls = jnp.sum(p, axis=-1, keepdims=True)
        os_ = jnp.dot(p.astype(BF16), v, preferred_element_type=F32)
        l = ls if l is None else l + ls
        o = os_ if o is None else o + os_
    if sink is not None:
        l = l + jnp.exp(sink - m)
    return o / l


def _half_select(lo, hi):
    lane = lax.broadcasted_iota(jnp.int32, lo.shape, 1)
    return jnp.where(lane < HEAD_DIM, lo, hi)


def _diff_lambda(lam_ref, init):
    lam = lam_ref[...]
    a = jnp.sum(lam[0:1, :] * lam[1:2, :], axis=-1, keepdims=True)
    b = jnp.sum(lam[2:3, :] * lam[3:4, :], axis=-1, keepdims=True)
    return jnp.exp(a) - jnp.exp(b) + init


def _sub_norm(o, gain_row, init):
    y = o * lax.rsqrt(jnp.mean(o * o, axis=-1, keepdims=True) + EPS)
    return (y * gain_row) * (1.0 - init)


A_NQ, A_NK, A_NV = 8, 2, 2
A_K0, A_KS0, A_V0, A_VS0 = 8, 10, 12, 14


def _a_head_cols(h):
    g = h // 4
    a, b = h % 2, g % 2
    kc = (A_K0 if a == b else A_KS0) + g // 2
    vc = (A_V0 if a == b else A_VS0) + g // 2
    return h // 2, a, kc, vc


def _a_ctx_kernel(qkv_ref, hm_ref, sink_ref, o_ref):
    for c in range(A_NQ):
        halves = []
        for a in range(2):
            h = 2 * c + a
            _, _, kc, vc = _a_head_cols(h)
            q = qkv_ref[:, c * LANES:(c + 1) * LANES] * hm_ref[a:a + 1, :]
            s = _qk(q, qkv_ref[:, kc * LANES:(kc + 1) * LANES])
            sink = sink_ref[h:h + 1, 0:1]
            halves.append(_softmax_pv([s], [qkv_ref[:, vc * LANES:(vc + 1) * LANES]], sink))
        o_ref[:, c * LANES:(c + 1) * LANES] = _half_select(halves[0], halves[1]).astype(BF16)


def _a_lat_kernel(q_ref, kv_ref, ck_ref, hm_ref, sink_ref, o_ref):
    n = pl.program_id(1)
    start = jnp.clip((n - 1) * QB, 0, DEC_SEQ - 3 * QB)
    start = pl.multiple_of(start, QB)
    q_pos = n * QB + lax.broadcasted_iota(jnp.int32, (QB, 3 * QB), 0)
    k_pos = start + lax.broadcasted_iota(jnp.int32, (QB, 3 * QB), 1)
    band = jnp.abs(q_pos - k_pos) <= A_WINDOW
    kv0 = A_K0
    for c in range(A_NQ):
        halves = []
        for a in range(2):
            h = 2 * c + a
            _, _, kc, vc = _a_head_cols(h)
            q = q_ref[:, c * LANES:(c + 1) * LANES] * hm_ref[a:a + 1, :]
            k_loc = kv_ref[pl.ds(start, 3 * QB), (kc - kv0) * LANES:(kc - kv0 + 1) * LANES]
            v_loc = kv_ref[pl.ds(start, 3 * QB), (vc - kv0) * LANES:(vc - kv0 + 1) * LANES]
            s_loc = jnp.where(band, _qk(q, k_loc), NEG_INF)
            k_ctx = ck_ref[0, :, (kc - kv0) * LANES:(kc - kv0 + 1) * LANES]
            v_ctx = ck_ref[0, :, (vc - kv0) * LANES:(vc - kv0 + 1) * LANES]
            s_ctx = _qk(q, k_ctx)
            sink = sink_ref[h:h + 1, 0:1]
            halves.append(_softmax_pv([s_loc, s_ctx], [v_loc, v_ctx], sink))
        o_ref[:, c * LANES:(c + 1) * LANES] = _half_select(halves[0], halves[1]).astype(BF16)


BC_N = 8


def _b_ctx_kernel(qkv_ref, hm_ref, lam_ref, sg_ref, o_ref, *, init):
    lam = _diff_lambda(lam_ref, init)
    for h in range(BC_N):
        q = qkv_ref[:, h * LANES:(h + 1) * LANES]
        k = qkv_ref[:, (BC_N + h) * LANES:(BC_N + h + 1) * LANES]
        v = qkv_ref[:, (2 * BC_N + h) * LANES:(2 * BC_N + h + 1) * LANES]
        o1 = _softmax_pv([_qk(q * hm_ref[0:1, :], k)], [v])
        o2 = _softmax_pv([_qk(q * hm_ref[1:2, :], k)], [v])
        o_ref[:, h * LANES:(h + 1) * LANES] = _sub_norm(o1 - lam * o2, sg_ref[...], init).astype(BF16)


def _c_ctx_kernel(qkv_ref, hm_ref, o_ref):
    for c in range(BC_N):
        k = qkv_ref[:, (BC_N + c) * LANES:(BC_N + c + 1) * LANES]
        v = qkv_ref[:, (2 * BC_N + c) * LANES:(2 * BC_N + c + 1) * LANES]
        halves = []
        for a in range(2):
            q = qkv_ref[:, c * LANES:(c + 1) * LANES] * hm_ref[a:a + 1, :]
            halves.append(_softmax_pv([_qk(q, k)], [v]))
        o_ref[:, c * LANES:(c + 1) * LANES] = _half_select(halves[0], halves[1]).astype(BF16)


def _ctx_attention(kernel, qkv, extra, n_cols):
    const2 = lambda b: (0, 0)
    in_specs = [pl.BlockSpec((SEQ, n_cols), lambda b: (b, 0))]
    in_specs += [pl.BlockSpec(e.shape, const2) for e in extra]
    return pl.pallas_call(
        kernel,
        grid=(BATCH,),
        in_specs=in_specs,
        out_specs=pl.BlockSpec((SEQ, D_MODEL), lambda b: (b, 0)),
        out_shape=jax.ShapeDtypeStruct((N_CTX, D_MODEL), BF16),
        compiler_params=_cparams(("arbitrary",)),
        name="ctx_attention",
    )(qkv, *extra)


def _a_lat_attention(qkv, cache_kv, hm, sink_rows):
    nblk = DEC_SEQ // QB
    return pl.pallas_call(
        _a_lat_kernel,
        grid=(DEC_BATCH, nblk),
        in_specs=[
            pl.BlockSpec((QB, A_NQ * LANES), lambda b, n: (b * nblk + n, 0)),
            pl.BlockSpec((DEC_SEQ, 8 * LANES), lambda b, n: (b, 1)),
            pl.BlockSpec((1, PAST_LEN, 8 * LANES), lambda b, n: (b, 0, 0)),
            pl.BlockSpec(hm.shape, lambda b, n: (0, 0)),
            pl.BlockSpec(sink_rows.shape, lambda b, n: (0, 0)),
        ],
        out_specs=pl.BlockSpec((QB, D_MODEL), lambda b, n: (b * nblk + n, 0)),
        out_shape=jax.ShapeDtypeStruct((N_LAT, D_MODEL), BF16),
        compiler_params=_cparams(("arbitrary", "arbitrary")),
        name="a_lat_attention",
    )(qkv, qkv, cache_kv, hm, sink_rows)


B_TQ = 512
B_TK = 512


def _b_lat_kernel(q_ref, k_ref, v_ref, ck_ref, cv_ref, hm_ref, lam_ref, sg_ref, o_ref, *, init):
    lam = _diff_lambda(lam_ref, init)
    q = q_ref[...]
    q1 = q * hm_ref[0:1, :]
    q2 = q * hm_ref[1:2, :]

    def update(carry, k, v):
        new = []
        for (m, l, acc), qq in zip(carry, (q1, q2)):
            s = _qk(qq, k)
            m_new = jnp.maximum(m, jnp.max(s, axis=-1, keepdims=True))
            alpha = jnp.exp(m - m_new)
            p = jnp.exp(s - m_new)
            l = alpha * l + jnp.sum(p, axis=-1, keepdims=True)
            acc = alpha * acc + jnp.dot(p.astype(BF16), v, preferred_element_type=F32)
            new.append((m_new, l, acc))
        return tuple(new)

    def body(i, carry):
        off = pl.multiple_of(i * B_TK, B_TK)
        return update(carry, k_ref[pl.ds(off, B_TK), :], v_ref[pl.ds(off, B_TK), :])

    init_state = (jnp.full((B_TQ, 1), -jnp.inf, F32), jnp.zeros((B_TQ, 1), F32), jnp.zeros((B_TQ, LANES), F32))
    carry = lax.fori_loop(0, DEC_SEQ // B_TK, body, (init_state, init_state))
    carry = update(carry, ck_ref[0], cv_ref[0])
    (_, l1, acc1), (_, l2, acc2) = carry
    o = acc1 / l1 - lam * (acc2 / l2)
    o_ref[...] = _sub_norm(o, sg_ref[...], init).astype(BF16)


def _b_lat_attention(qkv, cache_k, cache_v, hm, lam_p, sub_gain, init):
    nq = DEC_SEQ // B_TQ
    return pl.pallas_call(
        functools.partial(_b_lat_kernel, init=init),
        grid=(DEC_BATCH, BC_N, nq),
        in_specs=[
            pl.BlockSpec((B_TQ, LANES), lambda b, h, i: (b * nq + i, h)),
            pl.BlockSpec((DEC_SEQ, LANES), lambda b, h, i: (b, BC_N + h)),
            pl.BlockSpec((DEC_SEQ, LANES), lambda b, h, i: (b, 2 * BC_N + h)),
            pl.BlockSpec((1, PAST_LEN, LANES), lambda b, h, i: (b, 0, h)),
            pl.BlockSpec((1, PAST_LEN, LANES), lambda b, h, i: (b, 0, h)),
            pl.BlockSpec(hm.shape, lambda b, h, i: (0, 0)),
            pl.BlockSpec(lam_p.shape, lambda b, h, i: (0, 0)),
            pl.BlockSpec(sub_gain.shape, lambda b, h, i: (0, 0)),
        ],
        out_specs=pl.BlockSpec((B_TQ, LANES), lambda b, h, i: (b * nq + i, h)),
        out_shape=jax.ShapeDtypeStruct((N_LAT, D_MODEL), BF16),
        compiler_params=_cparams(("arbitrary", "arbitrary", "arbitrary")),
        name="b_lat_attention",
    )(qkv, qkv, qkv, cache_k, cache_v, hm, lam_p, sub_gain)


C_ROWS = DEC_SEQ // GRID_W
C_KEYS = NA_ROWS * GRID_W


def _c_row_start(r):
    return jnp.clip(r - NA_ROWS // 2, 0, C_ROWS - NA_ROWS)


def _c_lat_kernel(q_ref, k_ref, v_ref, ck_ref, cv_ref, bias_ref, hm_ref, o_ref):
    r = pl.program_id(1)
    start = pl.multiple_of(_c_row_start(r) * GRID_W, GRID_W)
    for c in range(BC_N):
        k_loc = k_ref[pl.ds(start, C_KEYS), c * LANES:(c + 1) * LANES]
        v_loc = v_ref[pl.ds(start, C_KEYS), c * LANES:(c + 1) * LANES]
        k_ctx = ck_ref[0, :, c * LANES:(c + 1) * LANES]
        v_ctx = cv_ref[0, :, c * LANES:(c + 1) * LANES]
        halves = []
        for a in range(2):
            q = q_ref[:, c * LANES:(c + 1) * LANES] * hm_ref[a:a + 1, :]
            s_loc = _qk(q, k_loc) + bias_ref[2 * c + a, 0]
            s_ctx = _qk(q, k_ctx)
            halves.append(_softmax_pv([s_loc, s_ctx], [v_loc, v_ctx]))
        o_ref[:, c * LANES:(c + 1) * LANES] = _half_select(halves[0], halves[1]).astype(BF16)


def _c_lat_attention(qkv, cache_k, cache_v, bias, hm):
    n_heads = 2 * BC_N
    return pl.pallas_call(
        _c_lat_kernel,
        grid=(DEC_BATCH, C_ROWS),
        in_specs=[
            pl.BlockSpec((GRID_W, BC_N * LANES), lambda b, r: (b * C_ROWS + r, 0)),
            pl.BlockSpec((DEC_SEQ, BC_N * LANES), lambda b, r: (b, 1)),
            pl.BlockSpec((DEC_SEQ, BC_N * LANES), lambda b, r: (b, 2)),
            pl.BlockSpec((1, PAST_LEN, D_MODEL), lambda b, r: (b, 0, 0)),
            pl.BlockSpec((1, PAST_LEN, D_MODEL), lambda b, r: (b, 0, 0)),
            pl.BlockSpec((n_heads, 1, GRID_W, C_KEYS), lambda b, r: (0, r - _c_row_start(r), 0, 0)),
            pl.BlockSpec(hm.shape, lambda b, r: (0, 0)),
        ],
        out_specs=pl.BlockSpec((GRID_W, D_MODEL), lambda b, r: (b * C_ROWS + r, 0)),
        out_shape=jax.ShapeDtypeStruct((N_LAT, D_MODEL), BF16),
        compiler_params=_cparams(("arbitrary", "arbitrary")),
        name="c_lat_attention",
    )(qkv, qkv, qkv, cache_k, cache_v, bias, hm)


def _c_bias_table(rpb):
    cols = jnp.arange(GRID_W)
    c_start = jnp.clip(cols - NA_COLS // 2, 0, GRID_W - NA_COLS)
    col_mask = (cols[None, :] >= c_start[:, None]) & (cols[None, :] < c_start[:, None] + NA_COLS)
    dc = jnp.clip(cols[None, :] - cols[:, None] + NA_COLS - 1, 0, 2 * NA_COLS - 2)
    off = jnp.arange(NA_ROWS)
    dr = jnp.arange(NA_ROWS)[None, :] - off[:, None] + NA_ROWS - 1
    bias = rpb.astype(F32)[:, dr][:, :, :, dc]
    bias = jnp.where(col_mask[None, None, None], bias, NEG_INF)
    bias = jnp.transpose(bias, (0, 1, 3, 2, 4))
    return bias.reshape(rpb.shape[0], NA_ROWS, GRID_W, C_KEYS)


RT_LT = TM // LANES
N_HP = 2 * PEER_HEADS
BIG_IDX = 1e9

_CAND_GROUPS = (
    (0, 0, 8), (0, 8, 8), (1, 0, 8), (2, 0, 5), (3, 0, 4), (4, 0, 3), (5, 0, 2), (6, 0, 2), (7, 0, 2), (None, 0, 8))


def _route_kernel(x_ref, o_ref, mod_ref, wo_ref, g_ref, wq_ref, sk_ref,
                  xn_ref, xb_ref, it_ref, jt_ref, gt_ref,
                  q_scr, tv_scr, ti_scr, bs_scr, be_scr):
    x_new = x_ref[...] + mod_ref[0, 2:3, :] * jnp.dot(o_ref[...], wo_ref[...], preferred_element_type=F32)
    xn_ref[...] = x_new
    xb = _modnorm(x_new, g_ref[...], mod_ref[0, 4:5, :], mod_ref[0, 3:4, :]).astype(BF16)
    xb_ref[...] = xb
    q_scr[...] = lax.dot_general(wq_ref[...], xb, (((1,), (1,)), ((), ())),
                                 preferred_element_type=F32).astype(BF16)

    row_iota = lax.broadcasted_iota(jnp.int32, (PEER_NKEYS, LANES), 0).astype(F32)

    def level1(hp, _):
        s_all = jnp.dot(sk_ref[hp], q_scr[pl.ds(pl.multiple_of(hp * PEER_NKEYS, PEER_NKEYS), PEER_NKEYS), :],
                        preferred_element_type=F32)
        for lt in range(RT_LT):
            s = s_all[:, lt * LANES:(lt + 1) * LANES]
            for r in range(PEER_TOPK):
                m = jnp.max(s, axis=0, keepdims=True)
                idx = jnp.min(jnp.where(s == m, row_iota, BIG_IDX), axis=0, keepdims=True)
                tv_scr[hp, r:r + 1, lt * LANES:(lt + 1) * LANES] = m
                ti_scr[hp, r:r + 1, lt * LANES:(lt + 1) * LANES] = idx
                s = jnp.where(row_iota == idx, -jnp.inf, s)
        return 0

    lax.fori_loop(0, N_HP, level1, 0)

    sub8 = lax.broadcasted_iota(jnp.int32, (SUBLANES, LANES), 0).astype(F32)

    def level2(h, _):
        for lt in range(RT_LT):
            ls = slice(lt * LANES, (lt + 1) * LANES)
            v1, v2 = tv_scr[2 * h, :, ls], tv_scr[2 * h + 1, :, ls]
            i1, i2 = ti_scr[2 * h, :, ls], ti_scr[2 * h + 1, :, ls]
            cands, flats, exps = [], [], []
            for a, b0, nvalid in _CAND_GROUPS:
                if a is None:
                    c = v1[8:16, :] + v2[0:1, :]
                    e = i1[8:16, :] * PEER_NKEYS + i2[0:1, :]
                    f = (sub8 + 8.0) * PEER_TOPK
                else:
                    c = v1[a:a + 1, :] + v2[b0:b0 + 8, :]
                    e = i1[a:a + 1, :] * PEER_NKEYS + i2[b0:b0 + 8, :]
                    f = sub8 + float(a * PEER_TOPK + b0)
                if nvalid < SUBLANES:
                    c = jnp.where(sub8 < nvalid, c, -jnp.inf)
                cands.append(c); flats.append(f); exps.append(e)
            for r in range(PEER_TOPK):
                m = jnp.max(functools.reduce(jnp.maximum, cands), axis=0, keepdims=True)
                fsel = [jnp.where(c == m, f, BIG_IDX) for c, f in zip(cands, flats)]
                fmin = jnp.min(functools.reduce(jnp.minimum, fsel), axis=0, keepdims=True)
                hits = [f == fmin for f in flats]
                esel = [jnp.where(hit, e, -1.0) for hit, e in zip(hits, exps)]
                ebest = jnp.max(functools.reduce(jnp.maximum, esel), axis=0, keepdims=True)
                bs_scr[h, r:r + 1, ls] = m
                be_scr[h, r:r + 1, ls] = ebest
                cands = [jnp.where(hit, -jnp.inf, c) for hit, c in zip(hits, cands)]
        return 0

    lax.fori_loop(0, PEER_HEADS, level2, 0)

    for h in range(PEER_HEADS):
        s16 = bs_scr[h]
        p = jnp.exp(s16 - jnp.max(s16, axis=0, keepdims=True))
        bs_scr[h] = p / jnp.sum(p, axis=0, keepdims=True)
    for lt in range(RT_LT):
        ls = slice(lt * LANES, (lt + 1) * LANES)
        e_t = jnp.concatenate([be_scr[h, :, ls] for h in range(PEER_HEADS)], axis=0).T
        i_t = jnp.floor(e_t * (1.0 / PEER_NKEYS))
        it_ref[ls, :] = i_t.astype(jnp.int32)
        jt_ref[ls, :] = (e_t - i_t * PEER_NKEYS).astype(jnp.int32)
        gt_ref[ls, :] = jnp.concatenate([bs_scr[h, :, ls] for h in range(PEER_HEADS)], axis=0).T


def _peer_route(x_all, o_all, mod_l, wo_bf, g_ffn, wq_t, sk_bf):
    const2 = lambda i: (0, 0)
    tok = lambda i: (i, 0)

    def mod_map(i):
        return (jnp.where(i < CTX_TILES, 0, 1 + (i - CTX_TILES) // LAT_TILES_PER_BATCH), 0, 0)

    nslot = PEER_HEADS * PEER_TOPK
    return pl.pallas_call(
        _route_kernel,
        grid=(N_TOK // TM,),
        in_specs=[
            pl.BlockSpec((TM, D_MODEL), tok),
            pl.BlockSpec((TM, D_MODEL), tok),
            pl.BlockSpec((1, 6, D_MODEL), mod_map),
            pl.BlockSpec((D_MODEL, D_MODEL), const2),
            pl.BlockSpec((1, D_MODEL), const2),
            pl.BlockSpec((N_HP * PEER_NKEYS, D_MODEL), const2),
            pl.BlockSpec((N_HP, PEER_NKEYS, PEER_NKEYS), lambda i: (0, 0, 0)),
        ],
        out_specs=[
            pl.BlockSpec((TM, D_MODEL), tok),
            pl.BlockSpec((TM, D_MODEL), tok),
            pl.BlockSpec((TM, nslot), tok),
            pl.BlockSpec((TM, nslot), tok),
            pl.BlockSpec((TM, nslot), tok),
        ],
        out_shape=[
            jax.ShapeDtypeStruct((N_TOK, D_MODEL), F32),
            jax.ShapeDtypeStruct((N_TOK, D_MODEL), BF16),
            jax.ShapeDtypeStruct((N_TOK, nslot), jnp.int32),
            jax.ShapeDtypeStruct((N_TOK, nslot), jnp.int32),
            jax.ShapeDtypeStruct((N_TOK, nslot), F32),
        ],
        scratch_shapes=[
            pltpu.VMEM((N_HP * PEER_NKEYS, TM), BF16),
            pltpu.VMEM((N_HP, PEER_TOPK, TM), F32),
            pltpu.VMEM((N_HP, PEER_TOPK, TM), F32),
            pltpu.VMEM((PEER_HEADS, PEER_TOPK, TM), F32),
            pltpu.VMEM((PEER_HEADS, PEER_TOPK, TM), F32),
        ],
        compiler_params=_cparams(("arbitrary",)),
        name="peer_route",
    )(x_all, o_all, mod_l, wo_bf, g_ffn, wq_t, sk_bf)


def _peer_kernel(xb_ref, xn_ref, mod_ref, it_ref, jt_ref, gt_ref, ut_ref, v_ref, y_ref,
                 slab_scr, act_scr, acc_scr):
    c = pl.program_id(1)

    @pl.when(c == 0)
    def _build():
        sub = lax.broadcasted_iota(jnp.int32, (PEER_NKEYS, LANES), 0)

        def body(t, _):
            irow = it_ref[pl.ds(t, 1), :]
            jrow = jt_ref[pl.ds(t, 1), :]
            grow = gt_ref[pl.ds(t, 1), :]
            a_t = jnp.where(sub == irow, grow, 0.0).astype(BF16)
            b_t = jnp.where(sub == jrow, 1.0, 0.0).astype(BF16)
            slab = lax.dot_general(a_t, b_t, (((1,), (1,)), ((), ())), preferred_element_type=F32)
            slab_scr[pl.ds(pl.multiple_of(t * SLAB_PITCH, SUBLANES), PEER_NKEYS), :] = slab
            return 0

        lax.fori_loop(0, PM_TM, body, 0)
        acc_scr[...] = jnp.zeros_like(acc_scr)

    h = jnp.dot(xb_ref[...], ut_ref[...], preferred_element_type=F32)
    for ii in range(PM_ROWS):
        hh = h[:, ii * LANES:(ii + 1) * LANES]
        gate = slab_scr[pl.ds(c * PM_ROWS + ii, PM_TM, stride=SLAB_PITCH), :]
        act = 0.5 * hh * (1.0 + lax.erf(hh * SQRT_HALF))
        act_scr[:, ii * LANES:(ii + 1) * LANES] = (gate * act).astype(BF16)
    acc_scr[...] += jnp.dot(act_scr[...], v_ref[...], preferred_element_type=F32)

    @pl.when(c == pl.num_programs(1) - 1)
    def _finish():
        y_ref[...] = xn_ref[...] + mod_ref[0, 5:6, :] * acc_scr[...]


def _peer_dense(xb, xn, mod_l, it, jt, gt, u_t, v_bf):
    tiles_per_mod = (TM // PM_TM)
    nslot = PEER_HEADS * PEER_TOPK

    def mod_map(t, c):
        i = t // tiles_per_mod
        return (jnp.where(i < CTX_TILES, 0, 1 + (i - CTX_TILES) // LAT_TILES_PER_BATCH), 0, 0)

    tok = lambda t, c: (t, 0)
    return pl.pallas_call(
        _peer_kernel,
        grid=(N_TOK // PM_TM, PEER_EXPERTS // PM_EC),
        in_specs=[
            pl.BlockSpec((PM_TM, D_MODEL), tok),
            pl.BlockSpec((PM_TM, D_MODEL), tok),
            pl.BlockSpec((1, 6, D_MODEL), mod_map),
            pl.BlockSpec((PM_TM, nslot), tok),
            pl.BlockSpec((PM_TM, nslot), tok),
            pl.BlockSpec((PM_TM, nslot), tok),
            pl.BlockSpec((D_MODEL, PM_EC), lambda t, c: (0, c)),
            pl.BlockSpec((PM_EC, D_MODEL), lambda t, c: (c, 0)),
        ],
        out_specs=pl.BlockSpec((PM_TM, D_MODEL), tok),
        out_shape=jax.ShapeDtypeStruct((N_TOK, D_MODEL), F32),
        scratch_shapes=[
            pltpu.VMEM((PM_TM * SLAB_PITCH, LANES), F32),
            pltpu.VMEM((PM_TM, PM_EC), BF16),
            pltpu.VMEM((PM_TM, D_MODEL), F32),
        ],
        compiler_params=_cparams(("arbitrary", "arbitrary")),
        name="peer_dense",
    )(xb, xn, mod_l, it, jt, gt, u_t, v_bf)


def _rope_tables():
    pos = np.arange(DEC_SEQ)
    row = (pos // GRID_W).astype(np.float32)
    col = (pos % GRID_W).astype(np.float32)
    half = HEAD_DIM // 2
    inv_freq = (1.0 / (ROPE_THETA ** (np.arange(0, half, 2, dtype=np.float32) / half))).astype(np.float32)
    ang_r = row[:, None] * inv_freq[None, :]
    ang_c = col[:, None] * inv_freq[None, :]
    ang = jnp.asarray(np.concatenate([ang_r, ang_r, ang_c, ang_c], axis=-1))
    cos, sin = jnp.cos(ang), jnp.sin(ang)
    first = (np.arange(HEAD_DIM) % half) < (half // 2)
    sin_a = jnp.where(first, -sin, 0.0)
    sin_b = jnp.where(first, 0.0, sin)
    tile2 = lambda t: jnp.concatenate([t, t], axis=-1)
    return tile2(cos), tile2(sin_a), tile2(sin_b)


def _tile2(g):
    return jnp.concatenate([g, g]).reshape(1, LANES).astype(F32)


def _swap_halves(x):
    return jnp.concatenate([x[..., HEAD_DIM:], x[..., :HEAD_DIM]], axis=-1)


def kernel(x_prompt, x_sample, cache_a_k, cache_a_v, cache_b_k, cache_b_v, cache_c_k, cache_c_v, c, c_ctx,
           norm_mix, norm_ffn, ada_w, ada_b, a_w_qkv, a_w_o, a_q_norm, a_k_norm, a_sink,
           b_w_qkv, b_w_o, b_q_norm, b_k_norm, b_lambda, b_sub_norm,
           c_w_qkv, c_w_o, c_q_norm, c_k_norm, c_rpb,
           peer_w_query, peer_sub_keys, peer_u, peer_v):
    x_all = jnp.concatenate([x_prompt.reshape(N_CTX, D_MODEL), x_sample.reshape(N_LAT, D_MODEL)], axis=0)
    cond8 = jnp.concatenate([c_ctx[None, :], c, jnp.zeros((SUBLANES - 1 - DEC_BATCH, D_MODEL), F32)], axis=0)
    mod = _modulation(cond8, ada_w, ada_b).reshape(DEPTH, SUBLANES, 6, D_MODEL)

    lane = np.arange(LANES)
    half_mask = jnp.asarray(np.stack([lane < HEAD_DIM, lane >= HEAD_DIM]).astype(np.float32), BF16)
    block_diag = jnp.asarray((lane[:, None] // HEAD_DIM == lane[None, :] // HEAD_DIM).astype(np.float32))
    rope = _rope_tables()

    new_k = ([], [], [])
    new_v = ([], [], [])
    for l in range(DEPTH):
        kind, j = l % N_MIXERS, l // N_MIXERS
        mod_l = mod[l]
        g_mix = norm_mix[l].reshape(1, D_MODEL)
        if kind == 0:
            w_qkv, w_o, gq, gk = a_w_qkv[j], a_w_o[j], a_q_norm[j], a_k_norm[j]
            dims = dict(nq=A_NQ, nk=A_NK, nv=A_NV, swap=True)
        elif kind == 1:
            w_qkv, w_o, gq, gk = b_w_qkv[j], b_w_o[j], b_q_norm[j], b_k_norm[j]
            dims = dict(nq=BC_N, nk=BC_N, nv=BC_N, swap=False)
        else:
            w_qkv, w_o, gq, gk = c_w_qkv[j], c_w_o[j], c_q_norm[j], c_k_norm[j]
            dims = dict(nq=BC_N, nk=BC_N, nv=BC_N, swap=False)
        w_bf = w_qkv.astype(BF16)
        common = (mod_l, g_mix, w_bf, _tile2(gq), _tile2(gk), block_diag)
        qkv_c, k_f, v_f = _project(x_all, *common, None, latent=False, **dims)
        qkv_l = _project(x_all, *common, None if kind == 2 else rope, latent=True, **dims)

        if kind == 0:
            sink_rows = jnp.broadcast_to(a_sink[j].astype(F32)[:, None], (a_sink.shape[1], LANES))
            o_c = _ctx_attention(_a_ctx_kernel, qkv_c, (half_mask, sink_rows), qkv_c.shape[1])
            ck = cache_a_k[:, j].reshape(DEC_BATCH, PAST_LEN, A_NK * LANES)
            cv = cache_a_v[:, j].reshape(DEC_BATCH, PAST_LEN, A_NV * LANES)
            cache_kv = jnp.concatenate([ck, _swap_halves(ck.reshape(DEC_BATCH, PAST_LEN, A_NK, LANES)).reshape(ck.shape),
                                        cv, _swap_halves(cv.reshape(DEC_BATCH, PAST_LEN, A_NV, LANES)).reshape(cv.shape)],
                                       axis=-1).astype(BF16)
            o_l = _a_lat_attention(qkv_l, cache_kv, half_mask, sink_rows)
            new_k[0].append(k_f.reshape(BATCH, SEQ, 4, HEAD_DIM))
            new_v[0].append(v_f.reshape(BATCH, SEQ, 4, HEAD_DIM))
        elif kind == 1:
            init = 0.8 - 0.6 * math.exp(-0.3 * l)
            lam_p = b_lambda[j].astype(F32)
            sub_gain = b_sub_norm[j].reshape(1, LANES).astype(F32)
            o_c = _ctx_attention(functools.partial(_b_ctx_kernel, init=init), qkv_c,
                                 (half_mask, lam_p, sub_gain), qkv_c.shape[1])
            ck = cache_b_k[:, j].reshape(DEC_BATCH, PAST_LEN, D_MODEL).astype(BF16)
            cv = cache_b_v[:, j].reshape(DEC_BATCH, PAST_LEN, D_MODEL).astype(BF16)
            o_l = _b_lat_attention(qkv_l, ck, cv, half_mask, lam_p, sub_gain, init)
            new_k[1].append(k_f.reshape(BATCH, SEQ, 8, 2 * HEAD_DIM))
            new_v[1].append(v_f.reshape(BATCH, SEQ, 8, 2 * HEAD_DIM))
        else:
            o_c = _ctx_attention(_c_ctx_kernel, qkv_c, (half_mask,), qkv_c.shape[1])
            ck = cache_c_k[:, j].reshape(DEC_BATCH, PAST_LEN, D_MODEL).astype(BF16)
            cv = cache_c_v[:, j].reshape(DEC_BATCH, PAST_LEN, D_MODEL).astype(BF16)
            o_l = _c_lat_attention(qkv_l, ck, cv, _c_bias_table(c_rpb[j]), half_mask)
            new_k[2].append(k_f.reshape(BATCH, SEQ, 16, HEAD_DIM))
            new_v[2].append(v_f.reshape(BATCH, SEQ, 16, HEAD_DIM))

        o_all = jnp.concatenate([o_c, o_l], axis=0)
        wq_t = peer_w_query[l].T.astype(BF16)
        sk_bf = peer_sub_keys[l].reshape(N_HP, PEER_NKEYS, PEER_NKEYS).astype(BF16)
        x_new, xb, it, jt, gt = _peer_route(x_all, o_all, mod_l, w_o.astype(BF16),
                                            norm_ffn[l].reshape(1, D_MODEL), wq_t, sk_bf)
        x_all = _peer_dense(xb, x_new, mod_l, it, jt, gt, peer_u[l].T.astype(BF16), peer_v[l].astype(BF16))

    y_prompt = x_all[:N_CTX].reshape(BATCH, SEQ, D_MODEL)
    y_sample = x_all[N_CTX:].reshape(DEC_BATCH, DEC_SEQ, D_MODEL)
    stack = lambda xs: jnp.stack(xs, axis=1)
    return (y_prompt, y_sample, stack(new_k[0]), stack(new_v[0]), stack(new_k[1]), stack(new_v[1]),
            stack(new_k[2]), stack(new_v[2]))
```

```python
import functools
import math

import numpy as np
import jax
import jax.numpy as jnp
from jax import lax
from jax.experimental import pallas as pl
from jax.experimental.pallas import tpu as pltpu

F32 = jnp.float32
BF16 = jnp.bfloat16
HIGHEST = lax.Precision.HIGHEST

D_MODEL = 1024
BATCH = 32
SEQ = 256
DEPTH = 4
DEC_BATCH = 2
DEC_SEQ = 4096
PAST_LEN = 512
GRID_W = 64
N_MIXERS = 3
HEAD_DIM = 64
ROPE_THETA = 10000.0
EPS = 1e-6
NEG_INF = -1e30
A_WINDOW = 128
QB = 128
NA_ROWS = 8
NA_COLS = 16
PEER_HEADS = 8
PEER_NKEYS = 128
PEER_EXPERTS = PEER_NKEYS * PEER_NKEYS
PEER_TOPK = 16

N_CTX = BATCH * SEQ
N_LAT = DEC_BATCH * DEC_SEQ
N_TOK = N_CTX + N_LAT

LANES = 128
SUBLANES = 8
VMEM_LIMIT = 56 * 1024 * 1024

TM = 512
CTX_TILES = N_CTX // TM
LAT_TILES = N_LAT // TM
LAT_TILES_PER_BATCH = DEC_SEQ // TM

PM_TM = 256
PM_EC = 2048
PM_ROWS = PM_EC // PEER_NKEYS
SLAB_PITCH = 136
PM_UNROLL = 16

SQRT_HALF = float(np.sqrt(0.5))


def _cparams(sem):
    return pltpu.CompilerParams(dimension_semantics=sem, vmem_limit_bytes=VMEM_LIMIT)


MOD_TN = 1536


def _mod_kernel(c_ref, w_ref, b_ref, o_ref):
    c = c_ref[...]
    s = c * (1.0 / (1.0 + jnp.exp(-c)))
    o_ref[0] = jnp.dot(s, w_ref[0], precision=HIGHEST, preferred_element_type=F32) + b_ref[0]


def _modulation(cond8, ada_w, ada_b):
    nt = (6 * D_MODEL) // MOD_TN
    return pl.pallas_call(
        _mod_kernel,
        grid=(DEPTH, nt),
        in_specs=[
            pl.BlockSpec((SUBLANES, D_MODEL), lambda l, n: (0, 0)),
            pl.BlockSpec((1, D_MODEL, MOD_TN), lambda l, n: (l, 0, n)),
            pl.BlockSpec((1, 1, MOD_TN), lambda l, n: (l, 0, n)),
        ],
        out_specs=pl.BlockSpec((1, SUBLANES, MOD_TN), lambda l, n: (l, 0, n)),
        out_shape=jax.ShapeDtypeStruct((DEPTH, SUBLANES, 6 * D_MODEL), F32),
        compiler_params=_cparams(("arbitrary", "arbitrary")),
        name="modulation",
    )(cond8, ada_w, ada_b.reshape(DEPTH, 1, 6 * D_MODEL))


def _mod_row_of_tile(i, latent):
    return 1 + i // LAT_TILES_PER_BATCH if latent else 0


def _modnorm(x, gain, scale, shift):
    y = x * lax.rsqrt(jnp.mean(x * x, axis=-1, keepdims=True) + EPS)
    return (y * gain) * (1.0 + scale) + shift


def _proj_kernel(*refs, nq, nk, nv, swap, latent, use_rope):
    if use_rope:
        (x_ref, mod_ref, g_ref, w_ref, gq_ref, gk_ref, bd_ref, cos_ref, sa_ref, sb_ref, out_ref) = refs
    elif latent:
        (x_ref, mod_ref, g_ref, w_ref, gq_ref, gk_ref, bd_ref, out_ref) = refs
    else:
        (x_ref, mod_ref, g_ref, w_ref, gq_ref, gk_ref, bd_ref, out_ref, kf_ref, vf_ref) = refs
    h = _modnorm(x_ref[...], g_ref[...], mod_ref[0, 1:2, :], mod_ref[0, 0:1, :])
    y = jnp.dot(h.astype(BF16), w_ref[...], preferred_element_type=F32)
    bd = bd_ref[...]
    scale = HEAD_DIM ** -0.5

    def qk_norm(c, gain_row):
        yc = y[:, c * LANES:(c + 1) * LANES]
        ss = jnp.dot(yc * yc, bd, precision=HIGHEST, preferred_element_type=F32)
        yn = yc * lax.rsqrt(ss * (1.0 / HEAD_DIM) + EPS) * gain_row
        if use_rope:
            yn = (yn * cos_ref[...] + pltpu.roll(yn, LANES - 16, 1) * sa_ref[...]
                  + pltpu.roll(yn, 16, 1) * sb_ref[...])
        return yn

    col = 0
    for c in range(nq):
        out_ref[:, col * LANES:(col + 1) * LANES] = (qk_norm(c, gq_ref[...]) * scale).astype(BF16)
        col += 1
    ks = [qk_norm(nq + c, gk_ref[...]) for c in range(nk)]
    vs = [y[:, (nq + nk + c) * LANES:(nq + nk + c + 1) * LANES] for c in range(nv)]
    groups = [ks] + ([[pltpu.roll(k, HEAD_DIM, 1) for k in ks]] if swap else [])
    groups += [vs] + ([[pltpu.roll(v, HEAD_DIM, 1) for v in vs]] if swap else [])
    for grp in groups:
        for a in grp:
            out_ref[:, col * LANES:(col + 1) * LANES] = a.astype(BF16)
            col += 1
    if not latent:
        for c in range(nk):
            kf_ref[:, c * LANES:(c + 1) * LANES] = ks[c]
        for c in range(nv):
            vf_ref[:, c * LANES:(c + 1) * LANES] = vs[c]


def _project(x_all, mod_l, gain, w_bf, gq2, gk2, bd, rope, *, nq, nk, nv, swap, latent):
    n_in = (nq + nk + nv) * LANES
    n_out = (nq + (2 if swap else 1) * (nk + nv)) * LANES
    tile0 = CTX_TILES if latent else 0
    const = lambda i: (0, 0)
    in_specs = [
        pl.BlockSpec((TM, D_MODEL), lambda i: (i + tile0, 0)),
        pl.BlockSpec((1, 6, D_MODEL), lambda i: (_mod_row_of_tile(i, latent), 0, 0)),
        pl.BlockSpec((1, D_MODEL), const),
        pl.BlockSpec((D_MODEL, n_in), const),
        pl.BlockSpec((1, LANES), const),
        pl.BlockSpec((1, LANES), const),
        pl.BlockSpec((LANES, LANES), const),
    ]
    args = [x_all, mod_l, gain, w_bf, gq2, gk2, bd]
    use_rope = rope is not None
    if use_rope:
        rope_spec = pl.BlockSpec((TM, LANES), lambda i: (i % LAT_TILES_PER_BATCH, 0))
        in_specs += [rope_spec] * 3
        args += list(rope)
    if latent:
        out_specs = pl.BlockSpec((TM, n_out), lambda i: (i, 0))
        out_shape = jax.ShapeDtypeStruct((N_LAT, n_out), BF16)
    else:
        out_specs = [pl.BlockSpec((TM, n_out), lambda i: (i, 0)),
                     pl.BlockSpec((TM, nk * LANES), lambda i: (i, 0)),
                     pl.BlockSpec((TM, nv * LANES), lambda i: (i, 0))]
        out_shape = [jax.ShapeDtypeStruct((N_CTX, n_out), BF16),
                     jax.ShapeDtypeStruct((N_CTX, nk * LANES), F32),
                     jax.ShapeDtypeStruct((N_CTX, nv * LANES), F32)]
    return pl.pallas_call(
        functools.partial(_proj_kernel, nq=nq, nk=nk, nv=nv, swap=swap, latent=latent, use_rope=use_rope),
        grid=(LAT_TILES if latent else CTX_TILES,),
        in_specs=in_specs, out_specs=out_specs, out_shape=out_shape,
        compiler_params=_cparams(("arbitrary",)),
        name="qkv_proj_lat" if latent else "qkv_proj_ctx",
    )(*args)


def _qk(q, k):
    return lax.dot_general(q, k, (((1,), (1,)), ((), ())), preferred_element_type=F32)


def _softmax_pv(logit_sets, value_sets, sink=None):
    m = None
    for s in logit_sets:
        r = jnp.max(s, axis=-1, keepdims=True)
        m = r if m is None else jnp.maximum(m, r)
    if sink is not None:
        m = jnp.maximum(m, sink)
    l = None
    o = None
    for s, v in zip(logit_sets, value_sets):
        p = jnp.exp(s - m)
        ls = jnp.sum(p, axis=-1, keepdims=True)
        os_ = jnp.dot(p.astype(BF16), v, preferred_element_type=F32)
        l = ls if l is None else l + ls
        o = os_ if o is None else o + os_
    if sink is not None:
        l = l + jnp.exp(sink - m)
    return o / l


def _half_select(lo, hi):
    lane = lax.broadcasted_iota(jnp.int32, lo.shape, 1)
    return jnp.where(lane < HEAD_DIM, lo, hi)


def _diff_lambda(lam_ref, init):
    lam = lam_ref[...]
    a = jnp.sum(lam[0:1, :] * lam[1:2, :], axis=-1, keepdims=True)
    b = jnp.sum(lam[2:3, :] * lam[3:4, :], axis=-1, keepdims=True)
    return jnp.exp(a) - jnp.exp(b) + init


def _sub_norm(o, gain_row, init):
    y = o * lax.rsqrt(jnp.mean(o * o, axis=-1, keepdims=True) + EPS)
    return (y * gain_row) * (1.0 - init)


A_NQ, A_NK, A_NV = 8, 2, 2
A_K0, A_KS0, A_V0, A_VS0 = 8, 10, 12, 14


def _a_head_cols(h):
    g = h // 4
    a, b = h % 2, g % 2
    kc = (A_K0 if a == b else A_KS0) + g // 2
    vc = (A_V0 if a == b else A_VS0) + g // 2
    return h // 2, a, kc, vc


def _a_ctx_kernel(qkv_ref, hm_ref, sink_ref, o_ref):
    for c in range(A_NQ):
        halves = []
        for a in range(2):
            h = 2 * c + a
            _, _, kc, vc = _a_head_cols(h)
            q = qkv_ref[:, c * LANES:(c + 1) * LANES] * hm_ref[a:a + 1, :]
            s = _qk(q, qkv_ref[:, kc * LANES:(kc + 1) * LANES])
            sink = sink_ref[h:h + 1, 0:1]
            halves.append(_softmax_pv([s], [qkv_ref[:, vc * LANES:(vc + 1) * LANES]], sink))
        o_ref[:, c * LANES:(c + 1) * LANES] = _half_select(halves[0], halves[1]).astype(BF16)


def _a_lat_kernel(q_ref, kv_ref, ck_ref, hm_ref, sink_ref, o_ref):
    n = pl.program_id(1)
    start = jnp.clip((n - 1) * QB, 0, DEC_SEQ - 3 * QB)
    start = pl.multiple_of(start, QB)
    q_pos = n * QB + lax.broadcasted_iota(jnp.int32, (QB, 3 * QB), 0)
    k_pos = start + lax.broadcasted_iota(jnp.int32, (QB, 3 * QB), 1)
    band = jnp.abs(q_pos - k_pos) <= A_WINDOW
    kv0 = A_K0
    for c in range(A_NQ):
        halves = []
        for a in range(2):
            h = 2 * c + a
            _, _, kc, vc = _a_head_cols(h)
            q = q_ref[:, c * LANES:(c + 1) * LANES] * hm_ref[a:a + 1, :]
            k_loc = kv_ref[pl.ds(start, 3 * QB), (kc - kv0) * LANES:(kc - kv0 + 1) * LANES]
            v_loc = kv_ref[pl.ds(start, 3 * QB), (vc - kv0) * LANES:(vc - kv0 + 1) * LANES]
            s_loc = jnp.where(band, _qk(q, k_loc), NEG_INF)
            k_ctx = ck_ref[0, :, (kc - kv0) * LANES:(kc - kv0 + 1) * LANES]
            v_ctx = ck_ref[0, :, (vc - kv0) * LANES:(vc - kv0 + 1) * LANES]
            s_ctx = _qk(q, k_ctx)
            sink = sink_ref[h:h + 1, 0:1]
            halves.append(_softmax_pv([s_loc, s_ctx], [v_loc, v_ctx], sink))
        o_ref[:, c * LANES:(c + 1) * LANES] = _half_select(halves[0], halves[1]).astype(BF16)


BC_N = 8


def _b_ctx_kernel(qkv_ref, hm_ref, lam_ref, sg_ref, o_ref, *, init):
    lam = _diff_lambda(lam_ref, init)
    for h in range(BC_N):
        q = qkv_ref[:, h * LANES:(h + 1) * LANES]
        k = qkv_ref[:, (BC_N + h) * LANES:(BC_N + h + 1) * LANES]
        v = qkv_ref[:, (2 * BC_N + h) * LANES:(2 * BC_N + h + 1) * LANES]
        o1 = _softmax_pv([_qk(q * hm_ref[0:1, :], k)], [v])
        o2 = _softmax_pv([_qk(q * hm_ref[1:2, :], k)], [v])
        o_ref[:, h * LANES:(h + 1) * LANES] = _sub_norm(o1 - lam * o2, sg_ref[...], init).astype(BF16)


def _c_ctx_kernel(qkv_ref, hm_ref, o_ref):
    for c in range(BC_N):
        k = qkv_ref[:, (BC_N + c) * LANES:(BC_N + c + 1) * LANES]
        v = qkv_ref[:, (2 * BC_N + c) * LANES:(2 * BC_N + c + 1) * LANES]
        halves = []
        for a in range(2):
            q = qkv_ref[:, c * LANES:(c + 1) * LANES] * hm_ref[a:a + 1, :]
            halves.append(_softmax_pv([_qk(q, k)], [v]))
        o_ref[:, c * LANES:(c + 1) * LANES] = _half_select(halves[0], halves[1]).astype(BF16)


def _ctx_attention(kernel, qkv, extra, n_cols):
    const2 = lambda b: (0, 0)
    in_specs = [pl.BlockSpec((SEQ, n_cols), lambda b: (b, 0))]
    in_specs += [pl.BlockSpec(e.shape, const2) for e in extra]
    return pl.pallas_call(
        kernel,
        grid=(BATCH,),
        in_specs=in_specs,
        out_specs=pl.BlockSpec((SEQ, D_MODEL), lambda b: (b, 0)),
        out_shape=jax.ShapeDtypeStruct((N_CTX, D_MODEL), BF16),
        compiler_params=_cparams(("arbitrary",)),
        name="ctx_attention",
    )(qkv, *extra)


def _a_lat_attention(qkv, cache_kv, hm, sink_rows):
    nblk = DEC_SEQ // QB
    return pl.pallas_call(
        _a_lat_kernel,
        grid=(DEC_BATCH, nblk),
        in_specs=[
            pl.BlockSpec((QB, A_NQ * LANES), lambda b, n: (b * nblk + n, 0)),
            pl.BlockSpec((DEC_SEQ, 8 * LANES), lambda b, n: (b, 1)),
            pl.BlockSpec((1, PAST_LEN, 8 * LANES), lambda b, n: (b, 0, 0)),
            pl.BlockSpec(hm.shape, lambda b, n: (0, 0)),
            pl.BlockSpec(sink_rows.shape, lambda b, n: (0, 0)),
        ],
        out_specs=pl.BlockSpec((QB, D_MODEL), lambda b, n: (b * nblk + n, 0)),
        out_shape=jax.ShapeDtypeStruct((N_LAT, D_MODEL), BF16),
        compiler_params=_cparams(("arbitrary", "arbitrary")),
        name="a_lat_attention",
    )(qkv, qkv, cache_kv, hm, sink_rows)


B_TQ = 512
B_TK = 512


def _b_lat_kernel(q_ref, k_ref, v_ref, ck_ref, cv_ref, hm_ref, lam_ref, sg_ref, o_ref, *, init):
    lam = _diff_lambda(lam_ref, init)
    q = q_ref[...]
    q1 = q * hm_ref[0:1, :]
    q2 = q * hm_ref[1:2, :]

    def update(carry, k, v):
        new = []
        for (m, l, acc), qq in zip(carry, (q1, q2)):
            s = _qk(qq, k)
            m_new = jnp.maximum(m, jnp.max(s, axis=-1, keepdims=True))
            alpha = jnp.exp(m - m_new)
            p = jnp.exp(s - m_new)
            l = alpha * l + jnp.sum(p, axis=-1, keepdims=True)
            acc = alpha * acc + jnp.dot(p.astype(BF16), v, preferred_element_type=F32)
            new.append((m_new, l, acc))
        return tuple(new)

    def body(i, carry):
        off = pl.multiple_of(i * B_TK, B_TK)
        return update(carry, k_ref[pl.ds(off, B_TK), :], v_ref[pl.ds(off, B_TK), :])

    init_state = (jnp.full((B_TQ, 1), -jnp.inf, F32), jnp.zeros((B_TQ, 1), F32), jnp.zeros((B_TQ, LANES), F32))
    carry = lax.fori_loop(0, DEC_SEQ // B_TK, body, (init_state, init_state))
    carry = update(carry, ck_ref[0], cv_ref[0])
    (_, l1, acc1), (_, l2, acc2) = carry
    o = acc1 / l1 - lam * (acc2 / l2)
    o_ref[...] = _sub_norm(o, sg_ref[...], init).astype(BF16)


def _b_lat_attention(qkv, cache_k, cache_v, hm, lam_p, sub_gain, init):
    nq = DEC_SEQ // B_TQ
    return pl.pallas_call(
        functools.partial(_b_lat_kernel, init=init),
        grid=(DEC_BATCH, BC_N, nq),
        in_specs=[
            pl.BlockSpec((B_TQ, LANES), lambda b, h, i: (b * nq + i, h)),
            pl.BlockSpec((DEC_SEQ, LANES), lambda b, h, i: (b, BC_N + h)),
            pl.BlockSpec((DEC_SEQ, LANES), lambda b, h, i: (b, 2 * BC_N + h)),
            pl.BlockSpec((1, PAST_LEN, LANES), lambda b, h, i: (b, 0, h)),
            pl.BlockSpec((1, PAST_LEN, LANES), lambda b, h, i: (b, 0, h)),
            pl.BlockSpec(hm.shape, lambda b, h, i: (0, 0)),
            pl.BlockSpec(lam_p.shape, lambda b, h, i: (0, 0)),
            pl.BlockSpec(sub_gain.shape, lambda b, h, i: (0, 0)),
        ],
        out_specs=pl.BlockSpec((B_TQ, LANES), lambda b, h, i: (b * nq + i, h)),
        out_shape=jax.ShapeDtypeStruct((N_LAT, D_MODEL), BF16),
        compiler_params=_cparams(("arbitrary", "arbitrary", "arbitrary")),
        name="b_lat_attention",
    )(qkv, qkv, qkv, cache_k, cache_v, hm, lam_p, sub_gain)


C_ROWS = DEC_SEQ // GRID_W
C_KEYS = NA_ROWS * GRID_W


def _c_row_start(r):
    return jnp.clip(r - NA_ROWS // 2, 0, C_ROWS - NA_ROWS)


def _c_lat_kernel(q_ref, k_ref, v_ref, ck_ref, cv_ref, bias_ref, hm_ref, o_ref):
    r = pl.program_id(1)
    start = pl.multiple_of(_c_row_start(r) * GRID_W, GRID_W)
    for c in range(BC_N):
        k_loc = k_ref[pl.ds(start, C_KEYS), c * LANES:(c + 1) * LANES]
        v_loc = v_ref[pl.ds(start, C_KEYS), c * LANES:(c + 1) * LANES]
        k_ctx = ck_ref[0, :, c * LANES:(c + 1) * LANES]
        v_ctx = cv_ref[0, :, c * LANES:(c + 1) * LANES]
        halves = []
        for a in range(2):
            q = q_ref[:, c * LANES:(c + 1) * LANES] * hm_ref[a:a + 1, :]
            s_loc = _qk(q, k_loc) + bias_ref[2 * c + a, 0]
            s_ctx = _qk(q, k_ctx)
            halves.append(_softmax_pv([s_loc, s_ctx], [v_loc, v_ctx]))
        o_ref[:, c * LANES:(c + 1) * LANES] = _half_select(halves[0], halves[1]).astype(BF16)


def _c_lat_attention(qkv, cache_k, cache_v, bias, hm):
    n_heads = 2 * BC_N
    return pl.pallas_call(
        _c_lat_kernel,
        grid=(DEC_BATCH, C_ROWS),
        in_specs=[
            pl.BlockSpec((GRID_W, BC_N * LANES), lambda b, r: (b * C_ROWS + r, 0)),
            pl.BlockSpec((DEC_SEQ, BC_N * LANES), lambda b, r: (b, 1)),
            pl.BlockSpec((DEC_SEQ, BC_N * LANES), lambda b, r: (b, 2)),
            pl.BlockSpec((1, PAST_LEN, D_MODEL), lambda b, r: (b, 0, 0)),
            pl.BlockSpec((1, PAST_LEN, D_MODEL), lambda b, r: (b, 0, 0)),
            pl.BlockSpec((n_heads, 1, GRID_W, C_KEYS), lambda b, r: (0, r - _c_row_start(r), 0, 0)),
            pl.BlockSpec(hm.shape, lambda b, r: (0, 0)),
        ],
        out_specs=pl.BlockSpec((GRID_W, D_MODEL), lambda b, r: (b * C_ROWS + r, 0)),
        out_shape=jax.ShapeDtypeStruct((N_LAT, D_MODEL), BF16),
        compiler_params=_cparams(("arbitrary", "arbitrary")),
        name="c_lat_attention",
    )(qkv, qkv, qkv, cache_k, cache_v, bias, hm)


def _c_bias_table(rpb):
    cols = jnp.arange(GRID_W)
    c_start = jnp.clip(cols - NA_COLS // 2, 0, GRID_W - NA_COLS)
    col_mask = (cols[None, :] >= c_start[:, None]) & (cols[None, :] < c_start[:, None] + NA_COLS)
    dc = jnp.clip(cols[None, :] - cols[:, None] + NA_COLS - 1, 0, 2 * NA_COLS - 2)
    off = jnp.arange(NA_ROWS)
    dr = jnp.arange(NA_ROWS)[None, :] - off[:, None] + NA_ROWS - 1
    bias = rpb.astype(F32)[:, dr][:, :, :, dc]
    bias = jnp.where(col_mask[None, None, None], bias, NEG_INF)
    bias = jnp.transpose(bias, (0, 1, 3, 2, 4))
    return bias.reshape(rpb.shape[0], NA_ROWS, GRID_W, C_KEYS)


RT_LT = TM // LANES
N_HP = 2 * PEER_HEADS
BIG_IDX = 1e9

_CAND_GROUPS = (
    (0, 0, 8), (0, 8, 8), (1, 0, 8), (2, 0, 5), (3, 0, 4), (4, 0, 3), (5, 0, 2), (6, 0, 2), (7, 0, 2), (None, 0, 8))


def _route_kernel(x_ref, o_ref, mod_ref, wo_ref, g_ref, wq_ref, sk_ref,
                  xn_ref, xb_ref, it_ref, jt_ref, gt_ref,
                  q_scr, tv_scr, ti_scr, bs_scr, be_scr):
    x_new = x_ref[...] + mod_ref[0, 2:3, :] * jnp.dot(o_ref[...], wo_ref[...], preferred_element_type=F32)
    xn_ref[...] = x_new
    xb = _modnorm(x_new, g_ref[...], mod_ref[0, 4:5, :], mod_ref[0, 3:4, :]).astype(BF16)
    xb_ref[...] = xb
    q_scr[...] = lax.dot_general(wq_ref[...], xb, (((1,), (1,)), ((), ())),
                                 preferred_element_type=F32).astype(BF16)

    row_iota = lax.broadcasted_iota(jnp.int32, (PEER_NKEYS, LANES), 0).astype(F32)

    def level1(hp, _):
        s_all = jnp.dot(sk_ref[hp], q_scr[pl.ds(pl.multiple_of(hp * PEER_NKEYS, PEER_NKEYS), PEER_NKEYS), :],
                        preferred_element_type=F32)
        for lt in range(RT_LT):
            s = s_all[:, lt * LANES:(lt + 1) * LANES]
            for r in range(PEER_TOPK):
                m = jnp.max(s, axis=0, keepdims=True)
                idx = jnp.min(jnp.where(s == m, row_iota, BIG_IDX), axis=0, keepdims=True)
                tv_scr[hp, r:r + 1, lt * LANES:(lt + 1) * LANES] = m
                ti_scr[hp, r:r + 1, lt * LANES:(lt + 1) * LANES] = idx
                s = jnp.where(row_iota == idx, -jnp.inf, s)
        return 0

    lax.fori_loop(0, N_HP, level1, 0)

    sub8 = lax.broadcasted_iota(jnp.int32, (SUBLANES, LANES), 0).astype(F32)

    def level2(h, _):
        for lt in range(RT_LT):
            ls = slice(lt * LANES, (lt + 1) * LANES)
            v1, v2 = tv_scr[2 * h, :, ls], tv_scr[2 * h + 1, :, ls]
            i1, i2 = ti_scr[2 * h, :, ls], ti_scr[2 * h + 1, :, ls]
            cands, flats, exps = [], [], []
            for a, b0, nvalid in _CAND_GROUPS:
                if a is None:
                    c = v1[8:16, :] + v2[0:1, :]
                    e = i1[8:16, :] * PEER_NKEYS + i2[0:1, :]
                    f = (sub8 + 8.0) * PEER_TOPK
                else:
                    c = v1[a:a + 1, :] + v2[b0:b0 + 8, :]
                    e = i1[a:a + 1, :] * PEER_NKEYS + i2[b0:b0 + 8, :]
                    f = sub8 + float(a * PEER_TOPK + b0)
                if nvalid < SUBLANES:
                    c = jnp.where(sub8 < nvalid, c, -jnp.inf)
                cands.append(c); flats.append(f); exps.append(e)
            for r in range(PEER_TOPK):
                m = jnp.max(functools.reduce(jnp.maximum, cands), axis=0, keepdims=True)
                fsel = [jnp.where(c == m, f, BIG_IDX) for c, f in zip(cands, flats)]
                fmin = jnp.min(functools.reduce(jnp.minimum, fsel), axis=0, keepdims=True)
                hits = [f == fmin for f in flats]
                esel = [jnp.where(hit, e, -1.0) for hit, e in zip(hits, exps)]
                ebest = jnp.max(functools.reduce(jnp.maximum, esel), axis=0, keepdims=True)
                bs_scr[h, r:r + 1, ls] = m
                be_scr[h, r:r + 1, ls] = ebest
                cands = [jnp.where(hit, -jnp.inf, c) for hit, c in zip(hits, cands)]
        return 0

    lax.fori_loop(0, PEER_HEADS, level2, 0)

    for h in range(PEER_HEADS):
        s16 = bs_scr[h]
        p = jnp.exp(s16 - jnp.max(s16, axis=0, keepdims=True))
        bs_scr[h] = p / jnp.sum(p, axis=0, keepdims=True)
    for lt in range(RT_LT):
        ls = slice(lt * LANES, (lt + 1) * LANES)
        e_t = jnp.concatenate([be_scr[h, :, ls] for h in range(PEER_HEADS)], axis=0).T
        i_t = jnp.floor(e_t * (1.0 / PEER_NKEYS))
        it_ref[ls, :] = i_t.astype(jnp.int32)
        jt_ref[ls, :] = (e_t - i_t * PEER_NKEYS).astype(jnp.int32)
        gt_ref[ls, :] = jnp.concatenate([bs_scr[h, :, ls] for h in range(PEER_HEADS)], axis=0).T


def _peer_route(x_all, o_all, mod_l, wo_bf, g_ffn, wq_t, sk_bf):
    const2 = lambda i: (0, 0)
    tok = lambda i: (i, 0)

    def mod_map(i):
        return (jnp.where(i < CTX_TILES, 0, 1 + (i - CTX_TILES) // LAT_TILES_PER_BATCH), 0, 0)

    nslot = PEER_HEADS * PEER_TOPK
    return pl.pallas_call(
        _route_kernel,
        grid=(N_TOK // TM,),
        in_specs=[
            pl.BlockSpec((TM, D_MODEL), tok),
            pl.BlockSpec((TM, D_MODEL), tok),
            pl.BlockSpec((1, 6, D_MODEL), mod_map),
            pl.BlockSpec((D_MODEL, D_MODEL), const2),
            pl.BlockSpec((1, D_MODEL), const2),
            pl.BlockSpec((N_HP * PEER_NKEYS, D_MODEL), const2),
            pl.BlockSpec((N_HP, PEER_NKEYS, PEER_NKEYS), lambda i: (0, 0, 0)),
        ],
        out_specs=[
            pl.BlockSpec((TM, D_MODEL), tok),
            pl.BlockSpec((TM, D_MODEL), tok),
            pl.BlockSpec((TM, nslot), tok),
            pl.BlockSpec((TM, nslot), tok),
            pl.BlockSpec((TM, nslot), tok),
        ],
        out_shape=[
            jax.ShapeDtypeStruct((N_TOK, D_MODEL), F32),
            jax.ShapeDtypeStruct((N_TOK, D_MODEL), BF16),
            jax.ShapeDtypeStruct((N_TOK, nslot), jnp.int32),
            jax.ShapeDtypeStruct((N_TOK, nslot), jnp.int32),
            jax.ShapeDtypeStruct((N_TOK, nslot), F32),
        ],
        scratch_shapes=[
            pltpu.VMEM((N_HP * PEER_NKEYS, TM), BF16),
            pltpu.VMEM((N_HP, PEER_TOPK, TM), F32),
            pltpu.VMEM((N_HP, PEER_TOPK, TM), F32),
            pltpu.VMEM((PEER_HEADS, PEER_TOPK, TM), F32),
            pltpu.VMEM((PEER_HEADS, PEER_TOPK, TM), F32),
        ],
        compiler_params=_cparams(("arbitrary",)),
        name="peer_route",
    )(x_all, o_all, mod_l, wo_bf, g_ffn, wq_t, sk_bf)


def _peer_kernel(xb_ref, xn_ref, mod_ref, it_ref, jt_ref, gt_ref, ut_ref, v_ref, y_ref,
                 slab_scr, act_scr, acc_scr):
    c = pl.program_id(1)

    @pl.when(c == 0)
    def _build():
        sub = lax.broadcasted_iota(jnp.int32, (PEER_NKEYS, LANES), 0)

        def body(t, _):
            irow = it_ref[pl.ds(t, 1), :]
            jrow = jt_ref[pl.ds(t, 1), :]
            grow = gt_ref[pl.ds(t, 1), :]
            a_t = jnp.where(sub == irow, grow, 0.0).astype(BF16)
            b_t = jnp.where(sub == jrow, 1.0, 0.0).astype(BF16)
            slab = lax.dot_general(a_t, b_t, (((1,), (1,)), ((), ())), preferred_element_type=F32)
            slab_scr[pl.ds(pl.multiple_of(t * SLAB_PITCH, SUBLANES), PEER_NKEYS), :] = slab
            return 0

        lax.fori_loop(0, PM_TM, body, 0, unroll=PM_UNROLL)

    h = jnp.dot(xb_ref[...], ut_ref[...], preferred_element_type=F32)
    for ii in range(PM_ROWS):
        hh = h[:, ii * LANES:(ii + 1) * LANES]
        gate = slab_scr[pl.ds(c * PM_ROWS + ii, PM_TM, stride=SLAB_PITCH), :]
        act = 0.5 * hh * (1.0 + lax.erf(hh * SQRT_HALF))
        act_scr[:, ii * LANES:(ii + 1) * LANES] = (gate * act).astype(BF16)
    prod = jnp.dot(act_scr[...], v_ref[...], preferred_element_type=F32)

    @pl.when(c == 0)
    def _assign():
        acc_scr[...] = prod

    @pl.when(c > 0)
    def _accumulate():
        acc_scr[...] += prod

    @pl.when(c == pl.num_programs(1) - 1)
    def _finish():
        y_ref[...] = xn_ref[...] + mod_ref[0, 5:6, :] * acc_scr[...]


def _peer_dense(xb, xn, mod_l, it, jt, gt, u_t, v_bf):
    tiles_per_mod = (TM // PM_TM)
    nslot = PEER_HEADS * PEER_TOPK

    def mod_map(t, c):
        i = t // tiles_per_mod
        return (jnp.where(i < CTX_TILES, 0, 1 + (i - CTX_TILES) // LAT_TILES_PER_BATCH), 0, 0)

    tok = lambda t, c: (t, 0)
    return pl.pallas_call(
        _peer_kernel,
        grid=(N_TOK // PM_TM, PEER_EXPERTS // PM_EC),
        in_specs=[
            pl.BlockSpec((PM_TM, D_MODEL), tok),
            pl.BlockSpec((PM_TM, D_MODEL), tok),
            pl.BlockSpec((1, 6, D_MODEL), mod_map),
            pl.BlockSpec((PM_TM, nslot), tok),
            pl.BlockSpec((PM_TM, nslot), tok),
            pl.BlockSpec((PM_TM, nslot), tok),
            pl.BlockSpec((D_MODEL, PM_EC), lambda t, c: (0, c)),
            pl.BlockSpec((PM_EC, D_MODEL), lambda t, c: (c, 0)),
        ],
        out_specs=pl.BlockSpec((PM_TM, D_MODEL), tok),
        out_shape=jax.ShapeDtypeStruct((N_TOK, D_MODEL), F32),
        scratch_shapes=[
            pltpu.VMEM((PM_TM * SLAB_PITCH, LANES), F32),
            pltpu.VMEM((PM_TM, PM_EC), BF16),
            pltpu.VMEM((PM_TM, D_MODEL), F32),
        ],
        compiler_params=_cparams(("arbitrary", "arbitrary")),
        name="peer_dense",
    )(xb, xn, mod_l, it, jt, gt, u_t, v_bf)


def _rope_tables():
    pos = np.arange(DEC_SEQ)
    row = (pos // GRID_W).astype(np.float32)
    col = (pos % GRID_W).astype(np.float32)
    half = HEAD_DIM // 2
    inv_freq = (1.0 / (ROPE_THETA ** (np.arange(0, half, 2, dtype=np.float32) / half))).astype(np.float32)
    ang_r = row[:, None] * inv_freq[None, :]
    ang_c = col[:, None] * inv_freq[None, :]
    ang = jnp.asarray(np.concatenate([ang_r, ang_r, ang_c, ang_c], axis=-1))
    cos, sin = jnp.cos(ang), jnp.sin(ang)
    first = (np.arange(HEAD_DIM) % half) < (half // 2)
    sin_a = jnp.where(first, -sin, 0.0)
    sin_b = jnp.where(first, 0.0, sin)
    tile2 = lambda t: jnp.concatenate([t, t], axis=-1)
    return tile2(cos), tile2(sin_a), tile2(sin_b)


def _tile2(g):
    return jnp.concatenate([g, g]).reshape(1, LANES).astype(F32)


def _swap_halves(x):
    return jnp.concatenate([x[..., HEAD_DIM:], x[..., :HEAD_DIM]], axis=-1)


def kernel(x_prompt, x_sample, cache_a_k, cache_a_v, cache_b_k, cache_b_v, cache_c_k, cache_c_v, c, c_ctx,
           norm_mix, norm_ffn, ada_w, ada_b, a_w_qkv, a_w_o, a_q_norm, a_k_norm, a_sink,
           b_w_qkv, b_w_o, b_q_norm, b_k_norm, b_lambda, b_sub_norm,
           c_w_qkv, c_w_o, c_q_norm, c_k_norm, c_rpb,
           peer_w_query, peer_sub_keys, peer_u, peer_v):
    x_all = jnp.concatenate([x_prompt.reshape(N_CTX, D_MODEL), x_sample.reshape(N_LAT, D_MODEL)], axis=0)
    cond8 = jnp.concatenate([c_ctx[None, :], c, jnp.zeros((SUBLANES - 1 - DEC_BATCH, D_MODEL), F32)], axis=0)
    mod = _modulation(cond8, ada_w, ada_b).reshape(DEPTH, SUBLANES, 6, D_MODEL)

    lane = np.arange(LANES)
    half_mask = jnp.asarray(np.stack([lane < HEAD_DIM, lane >= HEAD_DIM]).astype(np.float32), BF16)
    block_diag = jnp.asarray((lane[:, None] // HEAD_DIM == lane[None, :] // HEAD_DIM).astype(np.float32))
    rope = _rope_tables()

    new_k = ([], [], [])
    new_v = ([], [], [])
    for l in range(DEPTH):
        kind, j = l % N_MIXERS, l // N_MIXERS
        mod_l = mod[l]
        g_mix = norm_mix[l].reshape(1, D_MODEL)
        if kind == 0:
            w_qkv, w_o, gq, gk = a_w_qkv[j], a_w_o[j], a_q_norm[j], a_k_norm[j]
            dims = dict(nq=A_NQ, nk=A_NK, nv=A_NV, swap=True)
        elif kind == 1:
            w_qkv, w_o, gq, gk = b_w_qkv[j], b_w_o[j], b_q_norm[j], b_k_norm[j]
            dims = dict(nq=BC_N, nk=BC_N, nv=BC_N, swap=False)
        else:
            w_qkv, w_o, gq, gk = c_w_qkv[j], c_w_o[j], c_q_norm[j], c_k_norm[j]
            dims = dict(nq=BC_N, nk=BC_N, nv=BC_N, swap=False)
        w_bf = w_qkv.astype(BF16)
        common = (mod_l, g_mix, w_bf, _tile2(gq), _tile2(gk), block_diag)
        qkv_c, k_f, v_f = _project(x_all, *common, None, latent=False, **dims)
        qkv_l = _project(x_all, *common, None if kind == 2 else rope, latent=True, **dims)

        if kind == 0:
            sink_rows = jnp.broadcast_to(a_sink[j].astype(F32)[:, None], (a_sink.shape[1], LANES))
            o_c = _ctx_attention(_a_ctx_kernel, qkv_c, (half_mask, sink_rows), qkv_c.shape[1])
            ck = cache_a_k[:, j].reshape(DEC_BATCH, PAST_LEN, A_NK * LANES)
            cv = cache_a_v[:, j].reshape(DEC_BATCH, PAST_LEN, A_NV * LANES)
            cache_kv = jnp.concatenate([ck, _swap_halves(ck.reshape(DEC_BATCH, PAST_LEN, A_NK, LANES)).reshape(ck.shape),
                                        cv, _swap_halves(cv.reshape(DEC_BATCH, PAST_LEN, A_NV, LANES)).reshape(cv.shape)],
                                       axis=-1).astype(BF16)
            o_l = _a_lat_attention(qkv_l, cache_kv, half_mask, sink_rows)
            new_k[0].append(k_f.reshape(BATCH, SEQ, 4, HEAD_DIM))
            new_v[0].append(v_f.reshape(BATCH, SEQ, 4, HEAD_DIM))
        elif kind == 1:
            init = 0.8 - 0.6 * math.exp(-0.3 * l)
            lam_p = b_lambda[j].astype(F32)
            sub_gain = b_sub_norm[j].reshape(1, LANES).astype(F32)
            o_c = _ctx_attention(functools.partial(_b_ctx_kernel, init=init), qkv_c,
                                 (half_mask, lam_p, sub_gain), qkv_c.shape[1])
            ck = cache_b_k[:, j].reshape(DEC_BATCH, PAST_LEN, D_MODEL).astype(BF16)
            cv = cache_b_v[:, j].reshape(DEC_BATCH, PAST_LEN, D_MODEL).astype(BF16)
            o_l = _b_lat_attention(qkv_l, ck, cv, half_mask, lam_p, sub_gain, init)
            new_k[1].append(k_f.reshape(BATCH, SEQ, 8, 2 * HEAD_DIM))
            new_v[1].append(v_f.reshape(BATCH, SEQ, 8, 2 * HEAD_DIM))
        else:
            o_c = _ctx_attention(_c_ctx_kernel, qkv_c, (half_mask,), qkv_c.shape[1])
            ck = cache_c_k[:, j].reshape(DEC_BATCH, PAST_LEN, D_MODEL).astype(BF16)
            cv = cache_c_v[:, j].reshape(DEC_BATCH, PAST_LEN, D_MODEL).astype(BF16)
            o_l = _c_lat_attention(qkv_l, ck, cv, _c_bias_table(c_rpb[j]), half_mask)
            new_k[2].append(k_f.reshape(BATCH, SEQ, 16, HEAD_DIM))
            new_v[2].append(v_f.reshape(BATCH, SEQ, 16, HEAD_DIM))

        o_all = jnp.concatenate([o_c, o_l], axis=0)
        wq_t = peer_w_query[l].T.astype(BF16)
        sk_bf = peer_sub_keys[l].reshape(N_HP, PEER_NKEYS, PEER_NKEYS).astype(BF16)
        x_new, xb, it, jt, gt = _peer_route(x_all, o_all, mod_l, w_o.astype(BF16),
                                            norm_ffn[l].reshape(1, D_MODEL), wq_t, sk_bf)
        x_all = _peer_dense(xb, x_new, mod_l, it, jt, gt, peer_u[l].T.astype(BF16), peer_v[l].astype(BF16))

    y_prompt = x_all[:N_CTX].reshape(BATCH, SEQ, D_MODEL)
    y_sample = x_all[N_CTX:].reshape(DEC_BATCH, DEC_SEQ, D_MODEL)
    stack = lambda xs: jnp.stack(xs, axis=1)
    return (y_prompt, y_sample, stack(new_k[0]), stack(new_v[0]), stack(new_k[1]), stack(new_v[1]),
            stack(new_k[2]), stack(new_v[2]))
```
